```python
import math, functools
import jax
import jax.numpy as jnp
from jax import lax
import numpy as np

D_MODEL = 4096
BATCH = 2
SEQ = 4096
DEPTH = 2

CTX_LEN = 256
GRID_W = 64

DA_HEADS = 8
DA_DK = 128
DA_DV = 2 * DA_DK
GLA_HEADS = 8
GLA_DK = 128
GLA_DV = 256
GLA_GATE_RANK = 16
GLA_GATE_NORM = 16.0
HG_DK = 128
HG_HEADS = D_MODEL // HG_DK
HG_DV = D_MODEL // HG_HEADS

CHUNK = 64
Q_BLOCK = 128
ROPE_THETA = 10000.0
ROPE_AXIS_DIM = DA_DK // 2
D_FF = ((8 * D_MODEL // 3 + 255) // 256) * 256
CONV_W = 3
ALPHA = (2 * DEPTH) ** 0.25
BETA = (8 * DEPTH) ** -0.25
MOD_SCALE = 0.5
LN_EPS = 1e-5
RMS_EPS = 1e-5

EVEN_WIDTHS = (DA_HEADS * 2 * DA_DK, DA_HEADS * 2 * DA_DK, DA_HEADS * DA_DV,
               GLA_HEADS * GLA_DK, GLA_HEADS * GLA_DK, GLA_HEADS * GLA_DV, GLA_HEADS * GLA_DV,
               GLA_GATE_RANK, GLA_GATE_RANK)
EVEN_IN = sum(EVEN_WIDTHS)
EVEN_SPLIT = tuple(int(v) for v in np.cumsum(EVEN_WIDTHS)[:-1])
EVEN_MIX = DA_HEADS * DA_DV + GLA_HEADS * GLA_DV
ODD_IN = 5 * D_MODEL

kernel_name = 'hybrid_diffattn_gla_hgrn2_dit_trunk'


def layer_norm(x, g, b):
    xf = x.astype(jnp.float32)
    mu = jnp.mean(xf, axis=-1, keepdims=True)
    var = jnp.mean(jnp.square(xf - mu), axis=-1, keepdims=True)
    return ((xf - mu) * lax.rsqrt(var + LN_EPS) * g + b).astype(x.dtype)


def rms_norm(x, g):
    xf = x.astype(jnp.float32)
    return (xf * lax.rsqrt(jnp.mean(jnp.square(xf), axis=-1, keepdims=True) + RMS_EPS) * g).astype(x.dtype)


def modulation(cond, w_mod, b_mod):
    return jnp.split(jax.nn.silu(cond) @ w_mod + b_mod, 6, axis=-1)


def modulate(x, shift, scale):
    return x * (1.0 + scale) + shift


def axial_rope_angles(rows):
    row = jnp.repeat(jnp.arange(rows, dtype=jnp.float32), GRID_W)
    col = jnp.tile(jnp.arange(GRID_W, dtype=jnp.float32), rows)
    half = ROPE_AXIS_DIM // 2
    inv = ROPE_THETA ** (-jnp.arange(half, dtype=jnp.float32) / half)
    return row[:, None] * inv, col[:, None] * inv


def rotate(x, ang):
    a = ang[None, :, None, None, :]
    cos, sin = jnp.cos(a), jnp.sin(a)
    x1, x2 = jnp.split(x.astype(jnp.float32), 2, axis=-1)
    return jnp.concatenate([x1 * cos - x2 * sin, x2 * cos + x1 * sin], axis=-1).astype(x.dtype)


def axial_rope(x, angles):
    ang_r, ang_c = angles
    xr, xc = jnp.split(x, 2, axis=-1)
    return jnp.concatenate([rotate(xr, ang_r), rotate(xc, ang_c)], axis=-1)


def diff_lambda_init(layer_idx):
    return 0.8 - 0.6 * math.exp(-0.3 * layer_idx)


def lower_bound(table, layer_idx):
    s = jax.nn.softmax(table.astype(jnp.float32), axis=0)
    return (jnp.cumsum(s, axis=0) - s[0])[layer_idx]


def diff_attention(q_l, k_l, v_l, q_c, k_c, v_c, lam, need_ctx):
    B, T = q_l.shape[:2]
    k_all = jnp.concatenate([k_l, k_c], axis=1)
    v_all = jnp.concatenate([v_l, v_c], axis=1)

    def attend(q, k, v):
        s = jnp.einsum('bqhmd,bkhmd->bmhqk', q, k).astype(jnp.float32) * (DA_DK ** -0.5)
        p = jax.nn.softmax(s, axis=-1)
        a = (p[:, 0] - lam * p[:, 1]).astype(v.dtype)
        return jnp.einsum('bhqk,bkhe->bqhe', a, v)

    nb = T // Q_BLOCK
    q_blocks = jnp.moveaxis(q_l.reshape(B, nb, Q_BLOCK, DA_HEADS, 2, DA_DK), 1, 0)
    o_l = lax.map(lambda qb: attend(qb, k_all, v_all), q_blocks)
    o_l = jnp.moveaxis(o_l, 0, 1).reshape(B, T, DA_HEADS, DA_DV)
    o_c = attend(q_c, k_c, v_c) if need_ctx else None
    return o_l, o_c


def gla_chunk(q, k, v, g, s0):
    B, T, H, dk = q.shape
    dv = v.shape[-1]
    n = T // CHUNK
    blk = lambda a: a.astype(jnp.float32).reshape(B, n, CHUNK, H, a.shape[-1])
    q, k, v, g = blk(q), blk(k), blk(v), blk(g)
    b = jnp.cumsum(g, axis=2)
    b_mid = b[:, :, CHUNK // 2 - 1:CHUNK // 2]
    b_last = b[:, :, CHUNK - 1:]
    scores = jnp.einsum('bnchd,bnkhd->bnhck', q * jnp.exp(b - b_mid), k * jnp.exp(b_mid - b))
    tri = jnp.tril(jnp.ones((CHUNK, CHUNK), dtype=bool))
    scores = jnp.where(tri, scores, 0.0)
    o_intra = jnp.einsum('bnhck,bnkhe->bnche', scores, v)
    q_inter = q * jnp.exp(b)
    k_carry = k * jnp.exp(b_last - b)
    decay = jnp.exp(b_last[:, :, 0])

    def step(S, xs):
        q_n, k_n, v_n, d_n = xs
        o_n = jnp.einsum('bchd,bhde->bche', q_n, S)
        S = d_n[..., None] * S + jnp.einsum('bchd,bche->bhde', k_n, v_n)
        return S, o_n

    xs = tuple(jnp.moveaxis(a, 1, 0) for a in (q_inter, k_carry, v, decay))
    s_final, o_inter = lax.scan(step, s0, xs)
    o = o_intra + jnp.moveaxis(o_inter, 0, 1)
    return o.reshape(B, T, H, dv), s_final


def bidir_scan(lat, ctx, need_ctx):
    flip = lambda a: a[:, ::-1]
    qc, kfc, kbc, vc, gfc, gbc = ctx
    ql, kfl, kbl, vl, gfl, gbl = lat
    B, _, H, dk = qc.shape
    s0 = jnp.zeros((B, H, dk, vc.shape[-1]), jnp.float32)
    oc_f, sc_f = gla_chunk(qc, kfc, vc, gfc, s0)
    oc_b, sc_b = gla_chunk(flip(qc), flip(kbc), flip(vc), flip(gbc), s0)
    ol_f, _ = gla_chunk(ql, kfl, vl, gfl, sc_f)
    ol_b, _ = gla_chunk(flip(ql), flip(kbl), flip(vl), flip(gbl), sc_b)
    o_l = (ol_f + flip(ol_b)).astype(vl.dtype)
    o_c = (oc_f + flip(oc_b)).astype(vc.dtype) if need_ctx else None
    return o_l, o_c


def even_mixer(h_l, h_c, need_ctx, *, w_in, gla_wg_f, gla_bg_f, gla_wg_b, gla_bg_b,
               lq1, lk1, lq2, lk2, da_norm, gla_norm, w_out, angles, lam_init):
    def project(h):
        B, T = h.shape[:2]
        qa, ka, va, qb, kb, vb, gb, rf, rb = jnp.split(h @ w_in, EVEN_SPLIT, axis=-1)
        qa = qa.reshape(B, T, DA_HEADS, 2, DA_DK)
        ka = ka.reshape(B, T, DA_HEADS, 2, DA_DK)
        va = va.reshape(B, T, DA_HEADS, DA_DV)
        shp = (B, T, GLA_HEADS, GLA_DK)
        qb = qb.reshape(shp) * (GLA_DK ** -0.5)
        kb = kb.reshape(shp)
        vb = vb.reshape(B, T, GLA_HEADS, GLA_DV)
        g_f = (jax.nn.log_sigmoid((rf @ gla_wg_f + gla_bg_f).astype(jnp.float32)) / GLA_GATE_NORM).reshape(shp)
        g_b = (jax.nn.log_sigmoid((rb @ gla_wg_b + gla_bg_b).astype(jnp.float32)) / GLA_GATE_NORM).reshape(shp)
        return (qa, ka, va), (qb, kb, kb, vb, g_f, g_b), gb

    (qa_l, ka_l, va_l), gla_l, gate_l = project(h_l)
    (qa_c, ka_c, va_c), gla_c, gate_c = project(h_c)
    qa_l, ka_l = axial_rope(qa_l, angles), axial_rope(ka_l, angles)
    lam = (jnp.exp(jnp.sum(lq1.astype(jnp.float32) * lk1)) - jnp.exp(jnp.sum(lq2.astype(jnp.float32) * lk2))
           + lam_init)
    oa_l, oa_c = diff_attention(qa_l, ka_l, va_l, qa_c, ka_c, va_c, lam, need_ctx)
    ob_l, ob_c = bidir_scan(gla_l, gla_c, need_ctx)

    def readout(oa, ob, gate):
        B, T = oa.shape[:2]
        oa = rms_norm(oa, da_norm) * (1.0 - lam_init)
        ob = rms_norm(ob, gla_norm) * jax.nn.silu(gate).reshape(B, T, GLA_HEADS, GLA_DV)
        merged = jnp.concatenate([oa.reshape(B, T, -1), ob.reshape(B, T, -1)], axis=-1)
        return merged @ w_out

    out_l = readout(oa_l, ob_l, gate_l)
    out_c = readout(oa_c, ob_c, gate_c) if need_ctx else None
    return out_l, out_c


def odd_mixer(h_l, h_c, need_ctx, *, w_in, lb_f, lb_b, hg_norm, w_out):
    def project(h):
        B, T = h.shape[:2]
        q, i, zf, zb, g = jnp.split(h @ w_in, 5, axis=-1)
        shp = (B, T, HG_HEADS, HG_DK)
        q = (jax.nn.silu(q) * (HG_DK ** -0.5)).reshape(shp)

        def forget(z, lb):
            f = lb + (1.0 - lb) * jax.nn.sigmoid(z.astype(jnp.float32))
            return (1.0 - f).reshape(shp), jnp.log(f).reshape(shp)

        k_f, g_f = forget(zf, lb_f)
        k_b, g_b = forget(zb, lb_b)
        v = i.reshape(B, T, HG_HEADS, HG_DV)
        return (q, k_f, k_b, v, g_f, g_b), g

    scan_l, g_l = project(h_l)
    scan_c, g_c = project(h_c)
    o_l, o_c = bidir_scan(scan_l, scan_c, need_ctx)

    def readout(o, g):
        B, T = o.shape[:2]
        o = rms_norm(o, hg_norm) * jax.nn.silu(g).reshape(B, T, HG_HEADS, HG_DV)
        return o.reshape(B, T, D_MODEL) @ w_out

    out_l = readout(o_l, g_l)
    out_c = readout(o_c, g_c) if need_ctx else None
    return out_l, out_c


def conv_ffn(h, w_up, conv_w, conv_b, w_down):
    T = h.shape[1]
    u = h @ w_up
    pad = CONV_W // 2
    up = jnp.pad(u, ((0, 0), (pad, pad), (0, 0)))
    u = sum(conv_w[j] * up[:, j:j + T] for j in range(CONV_W)) + conv_b
    gate, val = jnp.split(u, 2, axis=-1)
    return (jax.nn.silu(gate) * val) @ w_down


def trunk_layer(xl, xc, c, c_ctx, mixer, w_mod, b_mod, ln1_g, ln1_b,
                ffn_up, ffn_conv, ffn_conv_b, ffn_down, ln2_g, ln2_b, need_ctx):
    sh_m, sc_m, gt_m, sh_f, sc_f, gt_f = [m[:, None, :] for m in modulation(c, w_mod, b_mod)]
    csh_m, csc_m, cgt_m, csh_f, csc_f, cgt_f = modulation(c_ctx, w_mod, b_mod)
    out_l, out_c = mixer(modulate(xl, sh_m, sc_m), modulate(xc, csh_m, csc_m), need_ctx)
    xl = layer_norm(ALPHA * xl + gt_m * out_l, ln1_g, ln1_b)
    f_l = conv_ffn(modulate(xl, sh_f, sc_f), ffn_up, ffn_conv, ffn_conv_b, ffn_down)
    xl = layer_norm(ALPHA * xl + gt_f * f_l, ln2_g, ln2_b)
    if need_ctx:
        xc = layer_norm(ALPHA * xc + cgt_m * out_c, ln1_g, ln1_b)
        f_c = conv_ffn(modulate(xc, csh_f, csc_f), ffn_up, ffn_conv, ffn_conv_b, ffn_down)
        xc = layer_norm(ALPHA * xc + cgt_f * f_c, ln2_g, ln2_b)
    return xl, xc


def setup_inputs(seed: int = 0) -> dict:
    key = jax.random.key(seed)
    keys = iter(jax.random.split(key, 64))

    def nrm(shape, scale):
        return jax.random.normal(next(keys), shape, jnp.float32) * scale

    def gain(n):
        return 1.0 + nrm((n,), 0.01)

    d = D_MODEL
    inp = {
        'x': nrm((BATCH, SEQ, d), 1.0),
        'c': nrm((BATCH, d), 1.0),
        'ctx': nrm((BATCH, CTX_LEN, d), 1.0),
        'c_ctx': nrm((d,), 1.0),
        'hgrn_lb_f': nrm((DEPTH, HG_HEADS * HG_DK), 0.5),
        'hgrn_lb_b': nrm((DEPTH, HG_HEADS * HG_DK), 0.5),
        'l0_w_mod': nrm((d, 6 * d), MOD_SCALE * d ** -0.5),
        'l0_b_mod': nrm((6 * d,), 0.01),
        'l0_w_in': nrm((d, EVEN_IN), d ** -0.5),
        'l0_gla_wg_f': nrm((GLA_GATE_RANK, GLA_HEADS * GLA_DK), GLA_GATE_RANK ** -0.5),
        'l0_gla_bg_f': nrm((GLA_HEADS * GLA_DK,), 0.1),
        'l0_gla_wg_b': nrm((GLA_GATE_RANK, GLA_HEADS * GLA_DK), GLA_GATE_RANK ** -0.5),
        'l0_gla_bg_b': nrm((GLA_HEADS * GLA_DK,), 0.1),
        'l0_da_lq1': nrm((DA_DK,), 0.1),
        'l0_da_lk1': nrm((DA_DK,), 0.1),
        'l0_da_lq2': nrm((DA_DK,), 0.1),
        'l0_da_lk2': nrm((DA_DK,), 0.1),
        'l0_da_norm': gain(DA_DV),
        'l0_gla_norm': gain(GLA_DV),
        'l0_w_out': nrm((EVEN_MIX, d), BETA * EVEN_MIX ** -0.5),
        'l0_ln1_g': gain(d),
        'l0_ln1_b': nrm((d,), 0.01),
        'l0_ffn_up': nrm((d, 2 * D_FF), d ** -0.5),
        'l0_ffn_conv': nrm((CONV_W, 2 * D_FF), CONV_W ** -0.5),
        'l0_ffn_conv_b': nrm((2 * D_FF,), 0.01),
        'l0_ffn_down': nrm((D_FF, d), BETA * D_FF ** -0.5),
        'l0_ln2_g': gain(d),
        'l0_ln2_b': nrm((d,), 0.01),
        'l1_w_mod': nrm((d, 6 * d), MOD_SCALE * d ** -0.5),
        'l1_b_mod': nrm((6 * d,), 0.01),
        'l1_w_in': nrm((d, ODD_IN), d ** -0.5),
        'l1_hg_norm': gain(HG_DV),
        'l1_w_out': nrm((d, d), BETA * d ** -0.5),
        'l1_ln1_g': gain(d),
        'l1_ln1_b': nrm((d,), 0.01),
        'l1_ffn_up': nrm((d, 2 * D_FF), d ** -0.5),
        'l1_ffn_conv': nrm((CONV_W, 2 * D_FF), CONV_W ** -0.5),
        'l1_ffn_conv_b': nrm((2 * D_FF,), 0.01),
        'l1_ffn_down': nrm((D_FF, d), BETA * D_FF ** -0.5),
        'l1_ln2_g': gain(d),
        'l1_ln2_b': nrm((d,), 0.01),
    }
    return inp


def reference(x, c, ctx, c_ctx, hgrn_lb_f, hgrn_lb_b,
              l0_w_mod, l0_b_mod, l0_w_in, l0_gla_wg_f, l0_gla_bg_f, l0_gla_wg_b, l0_gla_bg_b,
              l0_da_lq1, l0_da_lk1, l0_da_lq2, l0_da_lk2, l0_da_norm, l0_gla_norm, l0_w_out,
              l0_ln1_g, l0_ln1_b, l0_ffn_up, l0_ffn_conv, l0_ffn_conv_b, l0_ffn_down, l0_ln2_g, l0_ln2_b,
              l1_w_mod, l1_b_mod, l1_w_in, l1_hg_norm, l1_w_out, l1_ln1_g, l1_ln1_b,
              l1_ffn_up, l1_ffn_conv, l1_ffn_conv_b, l1_ffn_down, l1_ln2_g, l1_ln2_b):
    rows = x.shape[1] // GRID_W
    angles = axial_rope_angles(rows)
    even = functools.partial(
        even_mixer, w_in=l0_w_in, gla_wg_f=l0_gla_wg_f, gla_bg_f=l0_gla_bg_f,
        gla_wg_b=l0_gla_wg_b, gla_bg_b=l0_gla_bg_b, lq1=l0_da_lq1, lk1=l0_da_lk1,
        lq2=l0_da_lq2, lk2=l0_da_lk2, da_norm=l0_da_norm, gla_norm=l0_gla_norm,
        w_out=l0_w_out, angles=angles, lam_init=diff_lambda_init(0))
    odd = functools.partial(
        odd_mixer, w_in=l1_w_in, lb_f=lower_bound(hgrn_lb_f, 1), lb_b=lower_bound(hgrn_lb_b, 1),
        hg_norm=l1_hg_norm, w_out=l1_w_out)
    layers = (
        (even, (l0_w_mod, l0_b_mod, l0_ln1_g, l0_ln1_b, l0_ffn_up, l0_ffn_conv, l0_ffn_conv_b,
                l0_ffn_down, l0_ln2_g, l0_ln2_b)),
        (odd, (l1_w_mod, l1_b_mod, l1_ln1_g, l1_ln1_b, l1_ffn_up, l1_ffn_conv, l1_ffn_conv_b,
               l1_ffn_down, l1_ln2_g, l1_ln2_b)),
    )
    xl, xc = x, ctx
    for i in range(DEPTH):
        mixer, prm = layers[i]
        xl, xc = trunk_layer(xl, xc, c, c_ctx, mixer, *prm, need_ctx=(i < DEPTH - 1))
    return xl
```

```python
import functools
import math

import jax
import jax.numpy as jnp
from jax import lax
from jax.experimental import pallas as pl
from jax.experimental.pallas import tpu as pltpu

F32, BF16 = jnp.float32, jnp.bfloat16

D_MODEL = 4096
BATCH = 2
SEQ = 4096
DEPTH = 2
CTX_LEN = 256
GRID_W = 64
DA_HEADS = 8
DA_DK = 128
DA_DV = 2 * DA_DK
GLA_HEADS = 8
GLA_DK = 128
GLA_DV = 256
GLA_GATE_RANK = 16
GLA_GATE_NORM = 16.0
HG_DK = 128
HG_HEADS = D_MODEL // HG_DK
HG_DV = D_MODEL // HG_HEADS
CHUNK = 64
ROPE_THETA = 10000.0
ROPE_AXIS_DIM = DA_DK // 2
D_FF = ((8 * D_MODEL // 3 + 255) // 256) * 256
CONV_W = 3
ALPHA = (2 * DEPTH) ** 0.25
LN_EPS = 1e-5
RMS_EPS = 1e-5
LAM_INIT = 0.8 - 0.6 * math.exp(-0.3 * 0)

ATTN_COLS = DA_HEADS * (2 * DA_DK + 2 * DA_DK + DA_DV)
GLA_COLS = GLA_HEADS * (2 * GLA_DK + 2 * GLA_DV)
LANES = 128
BF16_ROWS = 16
HALO = BF16_ROWS
MOD_ROWS = 8
CTX_GROUP = BATCH
VMEM_LIMIT = 56 * 1024 * 1024


def _params(sem, vmem=VMEM_LIMIT):
    return pltpu.CompilerParams(dimension_semantics=sem, vmem_limit_bytes=vmem)


def _silu(x):
    return x * jax.nn.sigmoid(x)


def _log_sigmoid(x):
    return -(jnp.maximum(-x, 0.0) + jnp.log1p(jnp.exp(-jnp.abs(x))))


def _mod_spec(which, group_fn):
    return pl.BlockSpec((None, None, 1, D_MODEL), lambda i, *_: (group_fn(i), which, 0, 0))


def _vec_spec():
    return pl.BlockSpec((1, D_MODEL), lambda i, *_: (0, 0))


def _mod_kernel(c_ref, w_ref, b_ref, o_ref):
    s = _silu(c_ref[...]).astype(BF16)
    o_ref[...] = jnp.dot(s, w_ref[...].astype(BF16), preferred_element_type=F32) + b_ref[...]


def modulation_table(cond, w_mod, b_mod, tn=512):
    n = w_mod.shape[1]
    out = pl.pallas_call(
        _mod_kernel,
        out_shape=jax.ShapeDtypeStruct((MOD_ROWS, n), F32),
        grid=(n // tn,),
        in_specs=[pl.BlockSpec((MOD_ROWS, D_MODEL), lambda j: (0, 0)),
                  pl.BlockSpec((D_MODEL, tn), lambda j: (0, j)),
                  pl.BlockSpec((1, tn), lambda j: (0, j))],
        out_specs=pl.BlockSpec((MOD_ROWS, tn), lambda j: (0, j)),
        compiler_params=_params(("arbitrary",)),
        name="modulation",
    )(cond, w_mod, b_mod.reshape(1, n))
    return out.reshape(MOD_ROWS, 6, 1, D_MODEL)


def _modulate_kernel(x_ref, sh_ref, sc_ref, o_ref):
    o_ref[...] = (x_ref[...] * (1.0 + sc_ref[...]) + sh_ref[...]).astype(o_ref.dtype)


def modulate(x, mods, group_fn, tm=256):
    m = x.shape[0]
    return pl.pallas_call(
        _modulate_kernel,
        out_shape=jax.ShapeDtypeStruct((m, D_MODEL), BF16),
        grid=(m // tm,),
        in_specs=[pl.BlockSpec((tm, D_MODEL), lambda i: (i, 0)),
                  _mod_spec(0, lambda i: group_fn(i, tm)), _mod_spec(1, lambda i: group_fn(i, tm))],
        out_specs=pl.BlockSpec((tm, D_MODEL), lambda i: (i, 0)),
        compiler_params=_params(("arbitrary",)),
        name="modulate",
    )(x, mods, mods)


def _mm_kernel(a_ref, w_ref, o_ref):
    o_ref[...] = jnp.dot(a_ref[...], w_ref[...], preferred_element_type=F32).astype(o_ref.dtype)


def _mm_rope_kernel(a_ref, w_ref, cos_ref, sin_ref, o_ref, *, n_rope_tiles):
    acc = jnp.dot(a_ref[...], w_ref[...], preferred_element_type=F32)
    j = pl.program_id(1)
    tm, tn = acc.shape

    @pl.when(j < n_rope_tiles)
    def _():
        cos, sin = cos_ref[...], sin_ref[...]
        lane = lax.broadcasted_iota(jnp.int32, (tm, LANES), 1)
        first_half = (lane % (2 * (ROPE_AXIS_DIM // 2))) < (ROPE_AXIS_DIM // 2)
        for c in range(tn // LANES):
            xs = acc[:, c * LANES:(c + 1) * LANES]
            partner = jnp.where(first_half, pltpu.roll(xs, LANES - ROPE_AXIS_DIM // 2, 1),
                                pltpu.roll(xs, ROPE_AXIS_DIM // 2, 1))
            o_ref[:, c * LANES:(c + 1) * LANES] = (xs * cos + partner * sin).astype(o_ref.dtype)

    @pl.when(j >= n_rope_tiles)
    def _():
        o_ref[...] = acc.astype(o_ref.dtype)


def project(a, w, col0, n, out_dtype, tm, tn=512, rope=None, rope_cols=0):
    m, k = a.shape
    assert m % tm == 0 and n % tn == 0 and col0 % tn == 0
    in_specs = [pl.BlockSpec((tm, k), lambda i, j: (i, 0)),
                pl.BlockSpec((k, tn), lambda i, j: (0, col0 // tn + j))]
    args = [a, w]
    if rope is None:
        body = _mm_kernel
    else:
        assert rope_cols % tn == 0 and SEQ % tm == 0
        body = functools.partial(_mm_rope_kernel, n_rope_tiles=rope_cols // tn)
        tab = pl.BlockSpec((tm, LANES), lambda i, j: (i % (SEQ // tm), 0))
        in_specs += [tab, tab]
        args += list(rope)
    return pl.pallas_call(
        body,
        out_shape=jax.ShapeDtypeStruct((m, n), out_dtype),
        grid=(m // tm, n // tn),
        in_specs=in_specs,
        out_specs=pl.BlockSpec((tm, tn), lambda i, j: (i, j)),
        compiler_params=_params(("arbitrary", "arbitrary")),
        name="project",
    )(*args)


def _attn_kernel(*refs, n_src):
    lq1, lk1, lq2, lk2, norm_ref, q_ref = refs[:6]
    k_refs = refs[6:6 + n_src]
    v_refs = refs[6 + n_src:6 + 2 * n_src]
    o_ref = refs[6 + 2 * n_src]
    lam = (jnp.exp(jnp.sum(lq1[...] * lk1[...], axis=-1, keepdims=True))
           - jnp.exp(jnp.sum(lq2[...] * lk2[...], axis=-1, keepdims=True)) + LAM_INIT)
    scale = DA_DK ** -0.5
    probs = []
    for mp in range(2):
        qm = q_ref[:, mp * DA_DK:(mp + 1) * DA_DK]
        ss = [lax.dot_general(qm, kr[:, mp * DA_DK:(mp + 1) * DA_DK], (((1,), (1,)), ((), ())),
                              preferred_element_type=F32) for kr in k_refs]
        mx = ss[0].max(axis=-1, keepdims=True)
        for s in ss[1:]:
            mx = jnp.maximum(mx, s.max(axis=-1, keepdims=True))
        ps = [jnp.exp((s - mx) * scale) for s in ss]
        den = ps[0].sum(axis=-1, keepdims=True)
        for p in ps[1:]:
            den = den + p.sum(axis=-1, keepdims=True)
        probs.append((ps, 1.0 / den))
    (p1, inv1), (p2, inv2) = probs
    w2 = lam * inv2
    o = None
    for s in range(n_src):
        a = (p1[s] * inv1 - p2[s] * w2).astype(BF16)
        part = jnp.dot(a, v_refs[s][...], preferred_element_type=F32)
        o = part if o is None else o + part
    ms = jnp.mean(o * o, axis=-1, keepdims=True)
    o_ref[...] = (o * lax.rsqrt(ms + RMS_EPS) * norm_ref[...] * (1.0 - LAM_INIT)).astype(o_ref.dtype)


def diff_attention(q_src, kv_srcs, lam_vecs, da_norm, t_q, tq):
    nq = t_q // tq
    hq = 2 * DA_DK
    vec = pl.BlockSpec((1, DA_DK), lambda b, h, i: (0, 0))
    in_specs = [vec, vec, vec, vec, pl.BlockSpec((1, DA_DV), lambda b, h, i: (0, 0)),
                pl.BlockSpec((tq, hq), lambda b, h, i: (b * nq + i, h))]
    args = [v.reshape(1, DA_DK) for v in lam_vecs] + [da_norm.reshape(1, DA_DV), q_src]
    for off in (DA_HEADS, 2 * DA_HEADS):
        for arr, rows in kv_srcs:
            in_specs.append(pl.BlockSpec((rows, hq), lambda b, h, i, off=off: (b, off + h)))
            args.append(arr)
    return pl.pallas_call(
        functools.partial(_attn_kernel, n_src=len(kv_srcs)),
        out_shape=jax.ShapeDtypeStruct((BATCH * t_q, DA_HEADS * DA_DV), BF16),
        grid=(BATCH, DA_HEADS, nq),
        in_specs=in_specs,
        out_specs=pl.BlockSpec((tq, DA_DV), lambda b, h, i: (b * nq + i, h)),
        compiler_params=_params(("arbitrary", "arbitrary", "arbitrary")),
        name="diff_attention",
    )(*args)


def _chunk_step(q, k, v, g, tri, keep, mid, last, st_ref, emit):
    dk = q.shape[-1]
    g_hi = g.astype(BF16)
    r1 = g - g_hi.astype(F32)
    g_mid = r1.astype(BF16)
    g_lo = (r1 - g_mid.astype(F32)).astype(BF16)
    cs = jnp.dot(tri, jnp.concatenate([g_hi, g_mid, g_lo], axis=1), preferred_element_type=F32)
    bc = cs[:, :dk] + cs[:, dk:2 * dk] + cs[:, 2 * dk:]
    b_last = bc[last:last + 1, :]
    vb = v.astype(BF16)
    st = st_ref[...]
    o = None
    if emit:
        b_mid = bc[mid:mid + 1, :]
        qs = (q * jnp.exp(bc - b_mid)).astype(BF16)
        ks = (k * jnp.exp(b_mid - bc)).astype(BF16)
        sc = lax.dot_general(qs, ks, (((1,), (1,)), ((), ())), preferred_element_type=F32)
        sc = jnp.where(keep, sc, 0.0).astype(BF16)
        qi = (q * jnp.exp(bc)).astype(BF16)
        o = (jnp.dot(sc, vb, preferred_element_type=F32)
             + lax.dot_general(qi, st.astype(BF16), (((1,), (1,)), ((), ())), preferred_element_type=F32))
    k_carry = (k * jnp.exp(b_last - bc)).astype(BF16)
    upd = lax.dot_general(vb, k_carry, (((0,), (0,)), ((), ())), preferred_element_type=F32)
    st_ref[...] = jnp.exp(b_last) * st + upd
    return o


def _readout(o, gate, norm):
    ms = jnp.mean(o * o, axis=-1, keepdims=True)
    return o * lax.rsqrt(ms + RMS_EPS) * norm * _silu(gate)


def _scan_kernel(*refs, mode, n_lat, n_ctx, emit_ctx):
    n_stream = 5
    lat, ctx = refs[:n_stream], refs[n_stream:2 * n_stream]
    pos = 2 * n_stream
    if mode == "gla":
        wgf_ref, wgb_ref, bgf_ref, bgb_ref, norm_ref = refs[pos:pos + 5]
        pos += 5
    else:
        lbf_ref, lbb_ref, norm_ref = refs[pos:pos + 3]
        pos += 3
    out_lat = refs[pos]
    pos += 1
    out_ctx = None
    if emit_ctx:
        out_ctx = refs[pos]
        pos += 1
    st_ref, of_lat = refs[pos], refs[pos + 1]
    of_ctx = refs[pos + 2] if emit_ctx else None

    row = lax.broadcasted_iota(jnp.int32, (CHUNK, CHUNK), 0)
    col = lax.broadcasted_iota(jnp.int32, (CHUNK, CHUNK), 1)
    keep_f, keep_b = col <= row, col >= row
    tri_f = jnp.where(keep_f, 1.0, 0.0).astype(BF16)
    tri_b = jnp.where(keep_b, 1.0, 0.0).astype(BF16)
    dirs = {"f": (tri_f, keep_f, CHUNK // 2 - 1, CHUNK - 1), "b": (tri_b, keep_b, CHUNK // 2, 0)}

    if mode == "hgrn":
        def lower_bound(tab_ref):
            t = tab_ref[...]
            e = jnp.exp(t - jnp.max(t, axis=0, keepdims=True))
            s = e / jnp.sum(e, axis=0, keepdims=True)
            return (s[0:1] + s[1:2]) - s[0:1]
        lbs = {"f": lower_bound(lbf_ref), "b": lower_bound(lbb_ref)}

    def load(stream, rows, d):
        if mode == "gla":
            q_ref, k_ref, v_ref, _, r_ref = stream
            wg_ref, bg_ref = (wgf_ref, bgf_ref) if d == "f" else (wgb_ref, bgb_ref)
            q = q_ref[rows, :] * (GLA_DK ** -0.5)
            k = k_ref[rows, :]
            pre = jnp.dot(r_ref[rows, :].astype(BF16), wg_ref[...], preferred_element_type=F32) + bg_ref[...]
            g = _log_sigmoid(pre) * (1.0 / GLA_GATE_NORM)
        else:
            q_ref, v_ref, zf_ref, zb_ref, _ = stream
            z_ref = zf_ref if d == "f" else zb_ref
            q = _silu(q_ref[rows, :]) * (HG_DK ** -0.5)
            lb = lbs[d]
            f = lb + (1.0 - lb) * jax.nn.sigmoid(z_ref[rows, :])
            k = 1.0 - f
            g = jnp.log(f)
        return q, k, v_ref[rows, :], g

    def gate_of(stream):
        return stream[3] if mode == "gla" else stream[4]

    def run(stream, n_chunks, d, emit, of_ref, out_ref):
        tri, keep, mid, last = dirs[d]

        def body(c, carry):
            cc = c if d == "f" else n_chunks - 1 - c
            rows = pl.ds(pl.multiple_of(cc * CHUNK, CHUNK), CHUNK)
            q, k, v, g = load(stream, rows, d)
            o = _chunk_step(q, k, v, g, tri, keep, mid, last, st_ref, emit)
            if emit:
                if d == "f":
                    of_ref[rows, :] = o
                else:
                    out_ref[rows, :] = _readout(of_ref[rows, :] + o, gate_of(stream)[rows, :],
                                                norm_ref[...]).astype(out_ref.dtype)
            return carry

        lax.fori_loop(0, n_chunks, body, 0)

    for d in ("f", "b"):
        st_ref[...] = jnp.zeros_like(st_ref)
        run(ctx, n_ctx, d, emit_ctx, of_ctx, out_ctx)
        run(lat, n_lat, d, True, of_lat, out_lat)


def bidir_scan(mode, proj_lat, proj_ctx, extras, emit_ctx):
    if mode == "gla":
        heads, dk, dv = GLA_HEADS, GLA_DK, GLA_DV
        (pl_main, pl_r), (pc_main, pc_r) = proj_lat, proj_ctx
        qb, kb, vb, gb = 0, GLA_HEADS, GLA_HEADS, 2 * GLA_HEADS

        def stream(main, r, rows):
            specs = [pl.BlockSpec((rows, dk), lambda b, h: (b, qb + h)),
                     pl.BlockSpec((rows, dk), lambda b, h: (b, kb + h)),
                     pl.BlockSpec((rows, dv), lambda b, h: (b, vb + h)),
                     pl.BlockSpec((rows, dv), lambda b, h: (b, gb + h)),
                     pl.BlockSpec((rows, LANES), lambda b, h: (b, 0))]
            return specs, [main, main, main, main, r]

        wgf, wgb, bgf, bgb, norm = extras
        sq = pl.BlockSpec((None, LANES, dk), lambda b, h: (h, 0, 0))
        bias = pl.BlockSpec((None, 1, dk), lambda b, h: (h, 0, 0))
        extra_specs = [sq, sq, bias, bias, pl.BlockSpec((1, dv), lambda b, h: (0, 0))]
        extra_args = [wgf, wgb, bgf, bgb, norm.reshape(1, dv)]
    else:
        heads, dk, dv = HG_HEADS, HG_DK, HG_DV

        def stream(main, _, rows):
            specs = [pl.BlockSpec((rows, dk), lambda b, h, s=s: (b, s * HG_HEADS + h)) for s in range(5)]
            return specs, [main] * 5

        pl_main, pc_main = proj_lat, proj_ctx
        pl_r = pc_r = None
        lbf, lbb, norm = extras
        tab = pl.BlockSpec((DEPTH, dk), lambda b, h: (0, h))
        extra_specs = [tab, tab, pl.BlockSpec((1, dv), lambda b, h: (0, 0))]
        extra_args = [lbf, lbb, norm.reshape(1, dv)]

    ls, la = stream(pl_main, pl_r, SEQ)
    cs, ca = stream(pc_main, pc_r, CTX_LEN)
    out_shape = [jax.ShapeDtypeStruct((BATCH * SEQ, heads * dv), BF16)]
    out_specs = [pl.BlockSpec((SEQ, dv), lambda b, h: (b, h))]
    scratch = [pltpu.VMEM((dv, dk), F32), pltpu.VMEM((SEQ, dv), F32)]
    if emit_ctx:
        out_shape.append(jax.ShapeDtypeStruct((BATCH * CTX_LEN, heads * dv), BF16))
        out_specs.append(pl.BlockSpec((CTX_LEN, dv), lambda b, h: (b, h)))
        scratch.append(pltpu.VMEM((CTX_LEN, dv), F32))
    outs = pl.pallas_call(
        functools.partial(_scan_kernel, mode=mode, n_lat=SEQ // CHUNK, n_ctx=CTX_LEN // CHUNK, emit_ctx=emit_ctx),
        out_shape=out_shape,
        grid=(BATCH, heads),
        in_specs=ls + cs + extra_specs,
        out_specs=out_specs,
        scratch_shapes=scratch,
        compiler_params=_params(("arbitrary", "arbitrary")),
        name="bidir_scan_" + mode,
    )(*la, *ca, *extra_args)
    return outs if emit_ctx else (outs[0], None)


def _mm_ln_kernel(*refs, nk, with_h):
    if with_h:
        a_ref, w_ref, x_ref, gate_ref, g_ref, b_ref, sh_ref, sc_ref, xo_ref, ho_ref = refs
    else:
        a_ref, w_ref, x_ref, gate_ref, g_ref, b_ref, xo_ref = refs
    kk = pl.program_id(1)
    a = a_ref[...]
    tn = 512
    for n in range(D_MODEL // tn):
        cols = slice(n * tn, (n + 1) * tn)
        part = jnp.dot(a, w_ref[:, cols], preferred_element_type=F32)

        @pl.when(kk == 0)
        def _():
            xo_ref[:, cols] = part

        @pl.when(kk > 0)
        def _():
            xo_ref[:, cols] += part

    @pl.when(kk == nk - 1)
    def _():
        def body(r, carry):
            rows = pl.ds(pl.multiple_of(r * BF16_ROWS, BF16_ROWS), BF16_ROWS)
            y = ALPHA * x_ref[rows, :] + gate_ref[...] * xo_ref[rows, :]
            mu = jnp.mean(y, axis=-1, keepdims=True)
            dlt = y - mu
            var = jnp.mean(dlt * dlt, axis=-1, keepdims=True)
            z = dlt * lax.rsqrt(var + LN_EPS) * g_ref[...] + b_ref[...]
            xo_ref[rows, :] = z
            if with_h:
                ho_ref[rows, :] = (z * (1.0 + sc_ref[...]) + sh_ref[...]).astype(ho_ref.dtype)
            return carry

        lax.fori_loop(0, xo_ref.shape[0] // BF16_ROWS, body, 0)


def matmul_layernorm(a, w, x, gate, ln_g, ln_b, next_mod, group_fn, tm, tk):
    m, k = a.shape
    assert m % tm == 0 and k % tk == 0
    nk = k // tk
    gfn = lambda i: group_fn(i, tm)
    in_specs = [pl.BlockSpec((tm, tk), lambda i, kk: (i, kk)),
                pl.BlockSpec((tk, D_MODEL), lambda i, kk: (kk, 0)),
                pl.BlockSpec((tm, D_MODEL), lambda i, kk: (i, 0)),
                _mod_spec(gate[1], gfn), _vec_spec(), _vec_spec()]
    args = [a, w, x, gate[0], ln_g.reshape(1, D_MODEL), ln_b.reshape(1, D_MODEL)]
    out_shape = [jax.ShapeDtypeStruct((m, D_MODEL), F32)]
    out_specs = [pl.BlockSpec((tm, D_MODEL), lambda i, kk: (i, 0))]
    if next_mod is not None:
        in_specs += [_mod_spec(next_mod[1], gfn), _mod_spec(next_mod[2], gfn)]
        args += [next_mod[0], next_mod[0]]
        out_shape.append(jax.ShapeDtypeStruct((m, D_MODEL), BF16))
        out_specs.append(pl.BlockSpec((tm, D_MODEL), lambda i, kk: (i, 0)))
    outs = pl.pallas_call(
        functools.partial(_mm_ln_kernel, nk=nk, with_h=next_mod is not None),
        out_shape=out_shape,
        grid=(m // tm, nk),
        in_specs=in_specs,
        out_specs=out_specs,
        compiler_params=_params(("arbitrary", "arbitrary")),
        name="matmul_layernorm",
    )(*args)
    return (outs[0], outs[1]) if next_mod is not None else (outs[0], None)


def _ffn_up_kernel(a_ref, prev_ref, next_ref, wg_ref, wv_ref, cwg_ref, cwv_ref, cbg_ref, cbv_ref, o_ref,
                   aext_ref, *, tiles_per_seq):
    i, j = pl.program_id(0), pl.program_id(1)
    tm = a_ref.shape[0]

    @pl.when(j == 0)
    def _():
        first = (i % tiles_per_seq) == 0
        last = (i % tiles_per_seq) == tiles_per_seq - 1
        zero = jnp.zeros(prev_ref.shape, prev_ref.dtype)
        aext_ref[0:HALO, :] = jnp.where(first, zero, prev_ref[...])
        aext_ref[HALO:HALO + tm, :] = a_ref[...]
        aext_ref[HALO + tm:, :] = jnp.where(last, zero, next_ref[...])

    a = aext_ref[...]
    ext = tm + 2 * HALO

    def conv(w_ref, cw_ref, cb_ref):
        u = jnp.dot(a, w_ref[...], preferred_element_type=F32)
        up = pltpu.roll(u, 1, 0)[HALO:HALO + tm]
        un = pltpu.roll(u, ext - 1, 0)[HALO:HALO + tm]
        return cw_ref[0:1, :] * up + cw_ref[1:2, :] * u[HALO:HALO + tm] + cw_ref[2:3, :] * un + cb_ref[...]

    gate = conv(wg_ref, cwg_ref, cbg_ref)
    val = conv(wv_ref, cwv_ref, cbv_ref)
    o_ref[...] = (_silu(gate) * val).astype(o_ref.dtype)


def ffn_up(h, w_up, conv_w, conv_b, seq_len, tm, tn=256):
    m, k = h.shape
    assert seq_len % tm == 0 and D_FF % tn == 0 and tm % HALO == 0
    nj = D_FF // tn
    hb = tm // HALO
    last_hb = m // HALO - 1
    cb = conv_b.reshape(1, 2 * D_FF)
    return pl.pallas_call(
        functools.partial(_ffn_up_kernel, tiles_per_seq=seq_len // tm),
        out_shape=jax.ShapeDtypeStruct((m, D_FF), BF16),
        grid=(m // tm, nj),
        in_specs=[pl.BlockSpec((tm, k), lambda i, j: (i, 0)),
                  pl.BlockSpec((HALO, k), lambda i, j: (jnp.maximum(i * hb - 1, 0), 0)),
                  pl.BlockSpec((HALO, k), lambda i, j: (jnp.minimum((i + 1) * hb, last_hb), 0)),
                  pl.BlockSpec((k, tn), lambda i, j: (0, j)),
                  pl.BlockSpec((k, tn), lambda i, j: (0, nj + j)),
                  pl.BlockSpec((CONV_W, tn), lambda i, j: (0, j)),
                  pl.BlockSpec((CONV_W, tn), lambda i, j: (0, nj + j)),
                  pl.BlockSpec((1, tn), lambda i, j: (0, j)),
                  pl.BlockSpec((1, tn), lambda i, j: (0, nj + j))],
        out_specs=pl.BlockSpec((tm, tn), lambda i, j: (i, j)),
        scratch_shapes=[pltpu.VMEM((tm + 2 * HALO, k), BF16)],
        compiler_params=_params(("arbitrary", "arbitrary")),
        name="ffn_up",
    )(h, h, h, w_up, w_up, conv_w, conv_w, cb, cb)


def _rope_tables():
    rows = SEQ // GRID_W
    row = jnp.repeat(jnp.arange(rows, dtype=F32), GRID_W)
    col = jnp.tile(jnp.arange(GRID_W, dtype=F32), rows)
    half = ROPE_AXIS_DIM // 2
    inv = ROPE_THETA ** (-jnp.arange(half, dtype=F32) / half)
    ang_r, ang_c = row[:, None] * inv, col[:, None] * inv
    ang = jnp.concatenate([ang_r, ang_r, ang_c, ang_c], axis=-1)
    sign = jnp.tile(jnp.concatenate([-jnp.ones((half,), F32), jnp.ones((half,), F32)]), 2)
    return jnp.cos(ang), jnp.sin(ang) * sign


def _lat_group(i, tm):
    return i // (SEQ // tm)


def _ctx_group(i, tm):
    return CTX_GROUP


def _ffn(h_l, h_c, w_up, conv_w, conv_b):
    f_l = ffn_up(h_l, w_up, conv_w, conv_b, SEQ, tm=1024)
    f_c = ffn_up(h_c, w_up, conv_w, conv_b, CTX_LEN, tm=CTX_LEN) if h_c is not None else None
    return f_l, f_c


def kernel(x, c, ctx, c_ctx, hgrn_lb_f, hgrn_lb_b, l0_w_mod, l0_b_mod, l0_w_in, l0_gla_wg_f, l0_gla_bg_f, l0_gla_wg_b, l0_gla_bg_b, l0_da_lq1, l0_da_lk1, l0_da_lq2, l0_da_lk2, l0_da_norm, l0_gla_norm, l0_w_out, l0_ln1_g, l0_ln1_b, l0_ffn_up, l0_ffn_conv, l0_ffn_conv_b, l0_ffn_down, l0_ln2_g, l0_ln2_b, l1_w_mod, l1_b_mod, l1_w_in, l1_hg_norm, l1_w_out, l1_ln1_g, l1_ln1_b, l1_ffn_up, l1_ffn_conv, l1_ffn_conv_b, l1_ffn_down, l1_ln2_g, l1_ln2_b):
    xl = x.reshape(BATCH * SEQ, D_MODEL)
    xc = ctx.reshape(BATCH * CTX_LEN, D_MODEL)
    cond = jnp.concatenate([c, c_ctx[None, :], jnp.zeros((MOD_ROWS - BATCH - 1, D_MODEL), F32)], axis=0)
    mods0 = modulation_table(cond, l0_w_mod, l0_b_mod)
    mods1 = modulation_table(cond, l1_w_mod, l1_b_mod)
    bf = lambda w: w.astype(BF16)
    tm_l, tm_c = 512, 256

    w_in = bf(l0_w_in)
    h_l = modulate(xl, mods0, _lat_group)
    h_c = modulate(xc, mods0, _ctx_group)
    rope = _rope_tables()
    pa_l = project(h_l, w_in, 0, ATTN_COLS, BF16, tm_l, rope=rope, rope_cols=DA_HEADS * 4 * DA_DK)
    pa_c = project(h_c, w_in, 0, ATTN_COLS, BF16, tm_c)
    pb_l = project(h_l, w_in, ATTN_COLS, GLA_COLS, F32, tm_l)
    pb_c = project(h_c, w_in, ATTN_COLS, GLA_COLS, F32, tm_c)
    w_r = jnp.pad(w_in[:, ATTN_COLS + GLA_COLS:], ((0, 0), (0, LANES - 2 * GLA_GATE_RANK)))
    pr_l = project(h_l, w_r, 0, LANES, F32, tm_l, tn=LANES)
    pr_c = project(h_c, w_r, 0, LANES, F32, tm_c, tn=LANES)

    lam_vecs = (l0_da_lq1, l0_da_lk1, l0_da_lq2, l0_da_lk2)
    oa_l = diff_attention(pa_l, [(pa_l, SEQ), (pa_c, CTX_LEN)], lam_vecs, l0_da_norm, SEQ, tq=256)
    oa_c = diff_attention(pa_c, [(pa_c, CTX_LEN)], lam_vecs, l0_da_norm, CTX_LEN, tq=CTX_LEN)

    def gate_weight(wg, row0):
        w = bf(wg).reshape(GLA_GATE_RANK, GLA_HEADS, GLA_DK).transpose(1, 0, 2)
        return jnp.pad(w, ((0, 0), (row0, LANES - GLA_GATE_RANK - row0), (0, 0)))

    gla_extras = (gate_weight(l0_gla_wg_f, 0), gate_weight(l0_gla_wg_b, GLA_GATE_RANK),
                  l0_gla_bg_f.reshape(GLA_HEADS, 1, GLA_DK), l0_gla_bg_b.reshape(GLA_HEADS, 1, GLA_DK), l0_gla_norm)
    ob_l, ob_c = bidir_scan("gla", (pb_l, pr_l), (pb_c, pr_c), gla_extras, emit_ctx=True)

    w_out = bf(l0_w_out)
    x1_l, hf_l = matmul_layernorm(jnp.concatenate([oa_l, ob_l], axis=1), w_out, xl, (mods0, 2), l0_ln1_g, l0_ln1_b,
                                  (mods0, 3, 4), _lat_group, tm=tm_l, tk=512)
    x1_c, hf_c = matmul_layernorm(jnp.concatenate([oa_c, ob_c], axis=1), w_out, xc, (mods0, 2), l0_ln1_g, l0_ln1_b,
                                  (mods0, 3, 4), _ctx_group, tm=tm_c, tk=512)
    f_l, f_c = _ffn(hf_l, hf_c, bf(l0_ffn_up), l0_ffn_conv, l0_ffn_conv_b)
    w_down = bf(l0_ffn_down)
    x2_l, h1_l = matmul_layernorm(f_l, w_down, x1_l, (mods0, 5), l0_ln2_g, l0_ln2_b, (mods1, 0, 1), _lat_group,
                                  tm=tm_l, tk=256)
    _, h1_c = matmul_layernorm(f_c, w_down, x1_c, (mods0, 5), l0_ln2_g, l0_ln2_b, (mods1, 0, 1), _ctx_group,
                               tm=tm_c, tk=256)

    w_in1 = bf(l1_w_in)
    p1_l = project(h1_l, w_in1, 0, 5 * D_MODEL, F32, tm_l)
    p1_c = project(h1_c, w_in1, 0, 5 * D_MODEL, F32, tm_c)
    o1_l, _ = bidir_scan("hgrn", p1_l, p1_c, (hgrn_lb_f, hgrn_lb_b, l1_hg_norm), emit_ctx=False)
    x3_l, hf1_l = matmul_layernorm(o1_l, bf(l1_w_out), x2_l, (mods1, 2), l1_ln1_g, l1_ln1_b, (mods1, 3, 4),
                                   _lat_group, tm=tm_l, tk=512)
    f1_l, _ = _ffn(hf1_l, None, bf(l1_ffn_up), l1_ffn_conv, l1_ffn_conv_b)
    x4_l, _ = matmul_layernorm(f1_l, bf(l1_ffn_down), x3_l, (mods1, 5), l1_ln2_g, l1_ln2_b, None, _lat_group,
                               tm=tm_l, tk=256)
    return x4_l.reshape(BATCH, SEQ, D_MODEL)
```

```python
import functools
import math

import jax
import jax.numpy as jnp
from jax import lax
from jax.experimental import pallas as pl
from jax.experimental.pallas import tpu as pltpu

F32, BF16 = jnp.float32, jnp.bfloat16

D_MODEL = 4096
BATCH = 2
SEQ = 4096
DEPTH = 2
CTX_LEN = 256
GRID_W = 64
DA_HEADS = 8
DA_DK = 128
DA_DV = 2 * DA_DK
GLA_HEADS = 8
GLA_DK = 128
GLA_DV = 256
GLA_GATE_RANK = 16
GLA_GATE_NORM = 16.0
HG_DK = 128
HG_HEADS = D_MODEL // HG_DK
HG_DV = D_MODEL // HG_HEADS
CHUNK = 64
ROPE_THETA = 10000.0
ROPE_AXIS_DIM = DA_DK // 2
D_FF = ((8 * D_MODEL // 3 + 255) // 256) * 256
CONV_W = 3
ALPHA = (2 * DEPTH) ** 0.25
LN_EPS = 1e-5
RMS_EPS = 1e-5
LAM_INIT = 0.8 - 0.6 * math.exp(-0.3 * 0)

ATTN_COLS = DA_HEADS * (2 * DA_DK + 2 * DA_DK + DA_DV)
GLA_COLS = GLA_HEADS * (2 * GLA_DK + 2 * GLA_DV)
LANES = 128
BF16_ROWS = 16
HALO = BF16_ROWS
MOD_ROWS = 8
CTX_GROUP = BATCH
VMEM_LIMIT = 56 * 1024 * 1024
FFN_TN = 256
DOWN_TK = 512
SCAN_UNROLL = 4
D_FF_PAD = -(-D_FF // DOWN_TK) * DOWN_TK
Q_PRESCALE = DA_DK ** -0.5 * math.log2(math.e)


def _params(sem, vmem=VMEM_LIMIT):
    return pltpu.CompilerParams(dimension_semantics=sem, vmem_limit_bytes=vmem)


def _silu(x):
    return x * jax.nn.sigmoid(x)


def _log_sigmoid(x):
    return -(jnp.maximum(-x, 0.0) + jnp.log1p(jnp.exp(-jnp.abs(x))))


def _mod_spec(which, group_fn):
    return pl.BlockSpec((None, None, 1, D_MODEL), lambda i, *_: (group_fn(i), which, 0, 0))


def _vec_spec():
    return pl.BlockSpec((1, D_MODEL), lambda i, *_: (0, 0))


def _mod_kernel(c_ref, w_ref, b_ref, o_ref):
    s = _silu(c_ref[...]).astype(BF16)
    o_ref[...] = jnp.dot(s, w_ref[...].astype(BF16), preferred_element_type=F32) + b_ref[...]


def modulation_table(cond, w_mod, b_mod, tn=512):
    n = w_mod.shape[1]
    out = pl.pallas_call(
        _mod_kernel,
        out_shape=jax.ShapeDtypeStruct((MOD_ROWS, n), F32),
        grid=(n // tn,),
        in_specs=[pl.BlockSpec((MOD_ROWS, D_MODEL), lambda j: (0, 0)),
                  pl.BlockSpec((D_MODEL, tn), lambda j: (0, j)),
                  pl.BlockSpec((1, tn), lambda j: (0, j))],
        out_specs=pl.BlockSpec((MOD_ROWS, tn), lambda j: (0, j)),
        compiler_params=_params(("arbitrary",)),
        name="modulation",
    )(cond, w_mod, b_mod.reshape(1, n))
    return out.reshape(MOD_ROWS, 6, 1, D_MODEL)


def _modulate_kernel(x_ref, sh_ref, sc_ref, o_ref):
    o_ref[...] = (x_ref[...] * (1.0 + sc_ref[...]) + sh_ref[...]).astype(o_ref.dtype)


def modulate(x, mods, group_fn, tm=256):
    m = x.shape[0]
    return pl.pallas_call(
        _modulate_kernel,
        out_shape=jax.ShapeDtypeStruct((m, D_MODEL), BF16),
        grid=(m // tm,),
        in_specs=[pl.BlockSpec((tm, D_MODEL), lambda i: (i, 0)),
                  _mod_spec(0, lambda i: group_fn(i, tm)), _mod_spec(1, lambda i: group_fn(i, tm))],
        out_specs=pl.BlockSpec((tm, D_MODEL), lambda i: (i, 0)),
        compiler_params=_params(("arbitrary",)),
        name="modulate",
    )(x, mods, mods)


def _proj_kernel(*refs, n_q_tiles, n_rope_tiles):
    if n_rope_tiles:
        a_ref, w_ref, cos_ref, sin_ref, o_ref = refs
    else:
        a_ref, w_ref, o_ref = refs
    acc = jnp.dot(a_ref[...], w_ref[...], preferred_element_type=F32)
    j = pl.program_id(1)
    tm, tn = acc.shape
    if n_q_tiles:
        acc = acc * jnp.where(j < n_q_tiles, Q_PRESCALE, 1.0)
    if not n_rope_tiles:
        o_ref[...] = acc.astype(o_ref.dtype)
        return

    @pl.when(j < n_rope_tiles)
    def _():
        cos, sin = cos_ref[...], sin_ref[...]
        lane = lax.broadcasted_iota(jnp.int32, (tm, LANES), 1)
        first_half = (lane % (2 * (ROPE_AXIS_DIM // 2))) < (ROPE_AXIS_DIM // 2)
        for c in range(tn // LANES):
            xs = acc[:, c * LANES:(c + 1) * LANES]
            partner = jnp.where(first_half, pltpu.roll(xs, LANES - ROPE_AXIS_DIM // 2, 1),
                                pltpu.roll(xs, ROPE_AXIS_DIM // 2, 1))
            o_ref[:, c * LANES:(c + 1) * LANES] = (xs * cos + partner * sin).astype(o_ref.dtype)

    @pl.when(j >= n_rope_tiles)
    def _():
        o_ref[...] = acc.astype(o_ref.dtype)


def project(a, w, col0, n, out_dtype, tm, tn=512, q_cols=0, rope=None, rope_cols=0):
    m, k = a.shape
    assert m % tm == 0 and n % tn == 0 and col0 % tn == 0 and q_cols % tn == 0
    in_specs = [pl.BlockSpec((tm, k), lambda i, j: (i, 0)),
                pl.BlockSpec((k, tn), lambda i, j: (0, col0 // tn + j))]
    args = [a, w]
    if rope is not None:
        assert rope_cols % tn == 0 and SEQ % tm == 0
        tab = pl.BlockSpec((tm, LANES), lambda i, j: (i % (SEQ // tm), 0))
        in_specs += [tab, tab]
        args += list(rope)
    return pl.pallas_call(
        functools.partial(_proj_kernel, n_q_tiles=q_cols // tn, n_rope_tiles=rope_cols // tn),
        out_shape=jax.ShapeDtypeStruct((m, n), out_dtype),
        grid=(m // tm, n // tn),
        in_specs=in_specs,
        out_specs=pl.BlockSpec((tm, tn), lambda i, j: (i, j)),
        compiler_params=_params(("arbitrary", "arbitrary")),
        name="project",
    )(*args)


def _attn_kernel(*refs, n_src):
    lq1, lk1, lq2, lk2, norm_ref, q_ref = refs[:6]
    k_refs = refs[6:6 + n_src]
    v_refs = refs[6 + n_src:6 + 2 * n_src]
    o_ref = refs[6 + 2 * n_src]
    lam = (jnp.exp(jnp.sum(lq1[...] * lk1[...], axis=-1, keepdims=True))
           - jnp.exp(jnp.sum(lq2[...] * lk2[...], axis=-1, keepdims=True)) + LAM_INIT)
    probs = []
    for mp in range(2):
        qm = q_ref[:, mp * DA_DK:(mp + 1) * DA_DK]
        ss = [lax.dot_general(qm, kr[:, mp * DA_DK:(mp + 1) * DA_DK], (((1,), (1,)), ((), ())),
                              preferred_element_type=F32) for kr in k_refs]
        mx = ss[0].max(axis=-1, keepdims=True)
        for s in ss[1:]:
            mx = jnp.maximum(mx, s.max(axis=-1, keepdims=True))
        ps = [jnp.exp2(s - mx) for s in ss]
        den = ps[0].sum(axis=-1, keepdims=True)
        for p in ps[1:]:
            den = den + p.sum(axis=-1, keepdims=True)
        probs.append((ps, 1.0 / den))
    (p1, inv1), (p2, inv2) = probs
    w2 = lam * inv2
    o = None
    for s in range(n_src):
        a = (p1[s] * inv1 - p2[s] * w2).astype(BF16)
        part = jnp.dot(a, v_refs[s][...], preferred_element_type=F32)
        o = part if o is None else o + part
    ms = jnp.mean(o * o, axis=-1, keepdims=True)
    o_ref[...] = (o * lax.rsqrt(ms + RMS_EPS) * norm_ref[...] * (1.0 - LAM_INIT)).astype(o_ref.dtype)


def diff_attention(q_src, kv_srcs, lam_vecs, da_norm, t_q, tq):
    nq = t_q // tq
    hq = 2 * DA_DK
    vec = pl.BlockSpec((1, DA_DK), lambda b, h, i: (0, 0))
    in_specs = [vec, vec, vec, vec, pl.BlockSpec((1, DA_DV), lambda b, h, i: (0, 0)),
                pl.BlockSpec((tq, hq), lambda b, h, i: (b * nq + i, h))]
    args = [v.reshape(1, DA_DK) for v in lam_vecs] + [da_norm.reshape(1, DA_DV), q_src]
    for off in (DA_HEADS, 2 * DA_HEADS):
        for arr, rows in kv_srcs:
            in_specs.append(pl.BlockSpec((rows, hq), lambda b, h, i, off=off: (b, off + h)))
            args.append(arr)
    return pl.pallas_call(
        functools.partial(_attn_kernel, n_src=len(kv_srcs)),
        out_shape=jax.ShapeDtypeStruct((BATCH * t_q, DA_HEADS * DA_DV), BF16),
        grid=(BATCH, DA_HEADS, nq),
        in_specs=in_specs,
        out_specs=pl.BlockSpec((tq, DA_DV), lambda b, h, i: (b * nq + i, h)),
        compiler_params=_params(("arbitrary", "arbitrary", "arbitrary")),
        name="diff_attention",
    )(*args)


def _chunk_steps(jobs, st_refs, emit):
    nt = (((1,), (1,)), ((), ()))
    tn = (((0,), (0,)), ((), ()))
    for jb in jobs:
        g = jb["g"]
        g_hi = g.astype(BF16)
        r1 = g - g_hi.astype(F32)
        g_mid = r1.astype(BF16)
        g_lo = (r1 - g_mid.astype(F32)).astype(BF16)
        jb["cs"] = jnp.dot(jb["tri"], jnp.concatenate([g_hi, g_mid, g_lo], axis=1), preferred_element_type=F32)
    for jb in jobs:
        q, k, cs = jb["q"], jb["k"], jb["cs"]
        dk = q.shape[-1]
        bc = cs[:, :dk] + cs[:, dk:2 * dk] + cs[:, 2 * dk:]
        b_last = bc[jb["last"]:jb["last"] + 1, :]
        jb["decay"] = jnp.exp(b_last)
        k_carry = (k * jnp.exp(b_last - bc)).astype(BF16)
        jb["upd"] = lax.dot_general(jb["vb"], k_carry, tn, preferred_element_type=F32)
        if emit:
            b_mid = bc[jb["mid"]:jb["mid"] + 1, :]
            qs = (q * jnp.exp(bc - b_mid)).astype(BF16)
            ks = (k * jnp.exp(b_mid - bc)).astype(BF16)
            jb["sc"] = lax.dot_general(qs, ks, nt, preferred_element_type=F32)
            jb["qi"] = (q * jnp.exp(bc)).astype(BF16)
    st = {}
    for jb in jobs:
        ch = jb["chain"]
        if ch not in st:
            st[ch] = st_refs[ch][...]
        jb["st_in"] = st[ch]
        st[ch] = jb["decay"] * st[ch] + jb["upd"]
    for ch, val in st.items():
        st_refs[ch][...] = val
    if not emit:
        return None
    outs = []
    for jb in jobs:
        sc = jnp.where(jb["keep"], jb["sc"], 0.0).astype(BF16)
        outs.append(jnp.dot(sc, jb["vb"], preferred_element_type=F32)
                    + lax.dot_general(jb["qi"], jb["st_in"].astype(BF16), nt, preferred_element_type=F32))
    return outs


def _readout(o, gate, norm):
    ms = jnp.mean(o * o, axis=-1, keepdims=True)
    return o * lax.rsqrt(ms + RMS_EPS) * norm * _silu(gate)


def _scan_kernel(*refs, mode, hp, dk, dv, n_lat, n_ctx, emit_ctx):
    n_stream = 5
    lat, ctx = refs[:n_stream], refs[n_stream:2 * n_stream]
    pos = 2 * n_stream
    if mode == "gla":
        wgf_ref, wgb_ref, bgf_ref, bgb_ref, norm_ref = refs[pos:pos + 5]
        pos += 5
    else:
        lbf_ref, lbb_ref, norm_ref = refs[pos:pos + 3]
        pos += 3
    out_lat = refs[pos]
    pos += 1
    out_ctx = None
    if emit_ctx:
        out_ctx = refs[pos]
        pos += 1
    st_refs = refs[pos:pos + 2 * hp]
    pos += 2 * hp
    acc_lat = refs[pos]
    acc_ctx = refs[pos + 1] if emit_ctx else None

    row = lax.broadcasted_iota(jnp.int32, (CHUNK, CHUNK), 0)
    col = lax.broadcasted_iota(jnp.int32, (CHUNK, CHUNK), 1)
    keep_f, keep_b = col <= row, col >= row
    tri_f = jnp.where(keep_f, 1.0, 0.0).astype(BF16)
    tri_b = jnp.where(keep_b, 1.0, 0.0).astype(BF16)
    dirs = (("f", tri_f, keep_f, CHUNK // 2 - 1, CHUNK - 1), ("b", tri_b, keep_b, CHUNK // 2, 0))

    if mode == "hgrn":
        def lower_bound(tab_ref):
            t = tab_ref[...]
            e = jnp.exp(t - jnp.max(t, axis=0, keepdims=True))
            s = e / jnp.sum(e, axis=0, keepdims=True)
            return (s[0:1] + s[1:2]) - s[0:1]
        lbs = {"f": lower_bound(lbf_ref), "b": lower_bound(lbb_ref)}

    def load(stream, rows, d, h):
        kc = slice(h * dk, (h + 1) * dk)
        vc = slice(h * dv, (h + 1) * dv)
        if mode == "gla":
            q_ref, k_ref, v_ref, _, r_ref = stream
            wg_ref, bg_ref = (wgf_ref, bgf_ref) if d == "f" else (wgb_ref, bgb_ref)
            q = q_ref[rows, kc].astype(F32) * (dk ** -0.5)
            k = k_ref[rows, kc].astype(F32)
            pre = jnp.dot(r_ref[rows, :].astype(BF16), wg_ref[h], preferred_element_type=F32) + bg_ref[h]
            g = _log_sigmoid(pre) * (1.0 / GLA_GATE_NORM)
        else:
            q_ref, v_ref, zf_ref, zb_ref, _ = stream
            z_ref = zf_ref if d == "f" else zb_ref
            q = _silu(q_ref[rows, kc].astype(F32)) * (dk ** -0.5)
            lb = lbs[d][:, kc]
            f = lb + (1.0 - lb) * jax.nn.sigmoid(z_ref[rows, kc])
            k = 1.0 - f
            g = jnp.log(f)
        return q, k, v_ref[rows, vc], g

    def gate_of(stream):
        return stream[3] if mode == "gla" else stream[4]

    def run(stream, n_chunks, emit, acc_ref, out_ref):
        half = n_chunks // 2
        per_step = min(SCAN_UNROLL, half)

        def make_body(final):
            def body(c, carry):
                jobs = []
                for u in range(per_step):
                    step = c * per_step + u
                    for di, (d, tri, keep, mid, last) in enumerate(dirs):
                        cc = step if d == "f" else n_chunks - 1 - step
                        rows = pl.ds(pl.multiple_of(cc * CHUNK, CHUNK), CHUNK)
                        for h in range(hp):
                            q, k, vb, g = load(stream, rows, d, h)
                            jobs.append(dict(q=q, k=k, vb=vb, g=g, tri=tri, keep=keep, mid=mid, last=last,
                                             chain=di * hp + h, rows=rows, vc=slice(h * dv, (h + 1) * dv)))
                outs = _chunk_steps(jobs, st_refs, emit)
                if emit:
                    for jb, o in zip(jobs, outs):
                        rows, vc = jb["rows"], jb["vc"]
                        if final:
                            gate = gate_of(stream)[rows, vc].astype(F32)
                            out_ref[rows, vc] = _readout(acc_ref[rows, vc] + o, gate,
                                                         norm_ref[...]).astype(out_ref.dtype)
                        else:
                            acc_ref[rows, vc] = o
                return carry
            return body

        lax.fori_loop(0, half // per_step, make_body(False), 0)
        lax.fori_loop(half // per_step, n_chunks // per_step, make_body(True), 0)

    for st_ref in st_refs:
        st_ref[...] = jnp.zeros(st_ref.shape, F32)
    run(ctx, n_ctx, emit_ctx, acc_ctx, out_ctx)
    run(lat, n_lat, True, acc_lat, out_lat)


def bidir_scan(mode, proj_lat, proj_ctx, extras, emit_ctx, hp=2):
    if mode == "gla":
        heads, dk, dv = GLA_HEADS, GLA_DK, GLA_DV
        nb = heads // hp

        def stream(proj, rows):
            main, r = proj
            specs = [pl.BlockSpec((rows, hp * dk), lambda b, h: (b, h)),
                     pl.BlockSpec((rows, hp * dk), lambda b, h: (b, nb + h)),
                     pl.BlockSpec((rows, hp * dv), lambda b, h: (b, nb + h)),
                     pl.BlockSpec((rows, hp * dv), lambda b, h: (b, 2 * nb + h)),
                     pl.BlockSpec((rows, LANES), lambda b, h: (b, 0))]
            return specs, [main, main, main, main, r]

        wgf, wgb, bgf, bgb, norm = extras
        sq = pl.BlockSpec((hp, LANES, dk), lambda b, h: (h, 0, 0))
        bias = pl.BlockSpec((hp, 1, dk), lambda b, h: (h, 0, 0))
        extra_specs = [sq, sq, bias, bias, pl.BlockSpec((1, dv), lambda b, h: (0, 0))]
        extra_args = [wgf, wgb, bgf, bgb, norm.reshape(1, dv)]
    else:
        heads, dk, dv = HG_HEADS, HG_DK, HG_DV
        nb = heads // hp

        def stream(proj, rows):
            pq, pz, pg = proj
            blk = lambda off: pl.BlockSpec((rows, hp * dk), lambda b, h: (b, off + h))
            return [blk(0), blk(nb), blk(0), blk(nb), blk(0)], [pq, pq, pz, pz, pg]

        lbf, lbb, norm = extras
        tab = pl.BlockSpec((DEPTH, hp * dk), lambda b, h: (0, h))
        extra_specs = [tab, tab, pl.BlockSpec((1, dv), lambda b, h: (0, 0))]
        extra_args = [lbf, lbb, norm.reshape(1, dv)]

    ls, la = stream(proj_lat, SEQ)
    cs, ca = stream(proj_ctx, CTX_LEN)
    out_shape = [jax.ShapeDtypeStruct((BATCH * SEQ, heads * dv), BF16)]
    out_specs = [pl.BlockSpec((SEQ, hp * dv), lambda b, h: (b, h))]
    scratch = [pltpu.VMEM((dv, dk), F32) for _ in range(2 * hp)] + [pltpu.VMEM((SEQ, hp * dv), F32)]
    if emit_ctx:
        out_shape.append(jax.ShapeDtypeStruct((BATCH * CTX_LEN, heads * dv), BF16))
        out_specs.append(pl.BlockSpec((CTX_LEN, hp * dv), lambda b, h: (b, h)))
        scratch.append(pltpu.VMEM((CTX_LEN, hp * dv), F32))
    outs = pl.pallas_call(
        functools.partial(_scan_kernel, mode=mode, hp=hp, dk=dk, dv=dv, n_lat=SEQ // CHUNK,
                          n_ctx=CTX_LEN // CHUNK, emit_ctx=emit_ctx),
        out_shape=out_shape,
        grid=(BATCH, nb),
        in_specs=ls + cs + extra_specs,
        out_specs=out_specs,
        scratch_shapes=scratch,
        compiler_params=_params(("arbitrary", "arbitrary")),
        name="bidir_scan_" + mode,
    )(*la, *ca, *extra_args)
    return outs if emit_ctx else (outs[0], None)


def _mm_ln_kernel(*refs, nk, with_h):
    if with_h:
        a_ref, w_ref, x_ref, gate_ref, g_ref, b_ref, sh_ref, sc_ref, xo_ref, ho_ref = refs
    else:
        a_ref, w_ref, x_ref, gate_ref, g_ref, b_ref, xo_ref = refs
    kk = pl.program_id(1)

    @pl.when(kk == 0)
    def _():
        xo_ref[...] = jnp.zeros(xo_ref.shape, F32)

    a = a_ref[...]
    tn = 512
    for n in range(D_MODEL // tn):
        cols = slice(n * tn, (n + 1) * tn)
        xo_ref[:, cols] += jnp.dot(a, w_ref[:, cols], preferred_element_type=F32)

    @pl.when(kk == nk - 1)
    def _():
        def body(r, carry):
            rows = pl.ds(pl.multiple_of(r * BF16_ROWS, BF16_ROWS), BF16_ROWS)
            y = ALPHA * x_ref[rows, :] + gate_ref[...] * xo_ref[rows, :]
            mu = jnp.mean(y, axis=-1, keepdims=True)
            dlt = y - mu
            var = jnp.mean(dlt * dlt, axis=-1, keepdims=True)
            z = dlt * lax.rsqrt(var + LN_EPS) * g_ref[...] + b_ref[...]
            xo_ref[rows, :] = z
            if with_h:
                ho_ref[rows, :] = (z * (1.0 + sc_ref[...]) + sh_ref[...]).astype(ho_ref.dtype)
            return carry

        lax.fori_loop(0, xo_ref.shape[0] // BF16_ROWS, body, 0)


def matmul_layernorm(a, w, x, gate, ln_g, ln_b, next_mod, group_fn, tm, tk=DOWN_TK):
    m, k = a.shape
    assert m % tm == 0 and k % tk == 0
    nk = k // tk
    gfn = lambda i: group_fn(i, tm)
    in_specs = [pl.BlockSpec((tm, tk), lambda i, kk: (i, kk)),
                pl.BlockSpec((tk, D_MODEL), lambda i, kk: (kk, 0)),
                pl.BlockSpec((tm, D_MODEL), lambda i, kk: (i, 0)),
                _mod_spec(gate[1], gfn), _vec_spec(), _vec_spec()]
    args = [a, w, x, gate[0], ln_g.reshape(1, D_MODEL), ln_b.reshape(1, D_MODEL)]
    out_shape = [jax.ShapeDtypeStruct((m, D_MODEL), F32)]
    out_specs = [pl.BlockSpec((tm, D_MODEL), lambda i, kk: (i, 0))]
    if next_mod is not None:
        in_specs += [_mod_spec(next_mod[1], gfn), _mod_spec(next_mod[2], gfn)]
        args += [next_mod[0], next_mod[0]]
        out_shape.append(jax.ShapeDtypeStruct((m, D_MODEL), BF16))
        out_specs.append(pl.BlockSpec((tm, D_MODEL), lambda i, kk: (i, 0)))
    outs = pl.pallas_call(
        functools.partial(_mm_ln_kernel, nk=nk, with_h=next_mod is not None),
        out_shape=out_shape,
        grid=(m // tm, nk),
        in_specs=in_specs,
        out_specs=out_specs,
        compiler_params=_params(("arbitrary", "arbitrary")),
        name="matmul_layernorm",
    )(*args)
    return (outs[0], outs[1]) if next_mod is not None else (outs[0], None)


def _ffn_up_kernel(a_ref, prev_ref, next_ref, wg_ref, wv_ref, cwg_ref, cwv_ref, cbg_ref, cbv_ref, o_ref,
                   aext_ref, *, tiles_per_seq, nj):
    i, j = pl.program_id(0), pl.program_id(1)
    tm = a_ref.shape[0]
    ext = tm + 2 * HALO

    @pl.when(j == 0)
    def _():
        first = (i % tiles_per_seq) == 0
        last = (i % tiles_per_seq) == tiles_per_seq - 1
        zero = jnp.zeros(prev_ref.shape, prev_ref.dtype)
        aext_ref[0:HALO, :] = jnp.where(first, zero, prev_ref[...])
        aext_ref[HALO:HALO + tm, :] = a_ref[...]
        aext_ref[HALO + tm:, :] = jnp.where(last, zero, next_ref[...])

    @pl.when(j < nj)
    def _():
        a = aext_ref[...]

        def conv(w_ref, cw_ref, cb_ref):
            u = jnp.dot(a, w_ref[...], preferred_element_type=F32)
            up = pltpu.roll(u, 1, 0)[HALO:HALO + tm]
            un = pltpu.roll(u, ext - 1, 0)[HALO:HALO + tm]
            return cw_ref[0:1, :] * up + cw_ref[1:2, :] * u[HALO:HALO + tm] + cw_ref[2:3, :] * un + cb_ref[...]

        gate = conv(wg_ref, cwg_ref, cbg_ref)
        val = conv(wv_ref, cwv_ref, cbv_ref)
        o_ref[...] = (_silu(gate) * val).astype(o_ref.dtype)

    @pl.when(j >= nj)
    def _():
        o_ref[...] = jnp.zeros(o_ref.shape, o_ref.dtype)


def ffn_up(h, w_up, conv_w, conv_b, seq_len, tm, tn=FFN_TN):
    m, k = h.shape
    assert seq_len % tm == 0 and D_FF % tn == 0 and D_FF_PAD % tn == 0 and tm % HALO == 0
    nj = D_FF // tn
    hb = tm // HALO
    last_hb = m // HALO - 1
    cb = conv_b.reshape(1, 2 * D_FF)
    jc = lambda j: jnp.minimum(j, nj - 1)
    return pl.pallas_call(
        functools.partial(_ffn_up_kernel, tiles_per_seq=seq_len // tm, nj=nj),
        out_shape=jax.ShapeDtypeStruct((m, D_FF_PAD), BF16),
        grid=(m // tm, D_FF_PAD // tn),
        in_specs=[pl.BlockSpec((tm, k), lambda i, j: (i, 0)),
                  pl.BlockSpec((HALO, k), lambda i, j: (jnp.maximum(i * hb - 1, 0), 0)),
                  pl.BlockSpec((HALO, k), lambda i, j: (jnp.minimum((i + 1) * hb, last_hb), 0)),
                  pl.BlockSpec((k, tn), lambda i, j: (0, jc(j))),
                  pl.BlockSpec((k, tn), lambda i, j: (0, nj + jc(j))),
                  pl.BlockSpec((CONV_W, tn), lambda i, j: (0, jc(j))),
                  pl.BlockSpec((CONV_W, tn), lambda i, j: (0, nj + jc(j))),
                  pl.BlockSpec((1, tn), lambda i, j: (0, jc(j))),
                  pl.BlockSpec((1, tn), lambda i, j: (0, nj + jc(j)))],
        out_specs=pl.BlockSpec((tm, tn), lambda i, j: (i, j)),
        scratch_shapes=[pltpu.VMEM((tm + 2 * HALO, k), BF16)],
        compiler_params=_params(("arbitrary", "arbitrary")),
        name="ffn_up",
    )(h, h, h, w_up, w_up, conv_w, conv_w, cb, cb)


def _rope_tables():
    rows = SEQ // GRID_W
    row = jnp.repeat(jnp.arange(rows, dtype=F32), GRID_W)
    col = jnp.tile(jnp.arange(GRID_W, dtype=F32), rows)
    half = ROPE_AXIS_DIM // 2
    inv = ROPE_THETA ** (-jnp.arange(half, dtype=F32) / half)
    ang_r, ang_c = row[:, None] * inv, col[:, None] * inv
    ang = jnp.concatenate([ang_r, ang_r, ang_c, ang_c], axis=-1)
    sign = jnp.tile(jnp.concatenate([-jnp.ones((half,), F32), jnp.ones((half,), F32)]), 2)
    return jnp.cos(ang), jnp.sin(ang) * sign


def _lat_group(i, tm):
    return i // (SEQ // tm)


def _ctx_group(i, tm):
    return CTX_GROUP


def _ffn(h_l, h_c, w_up, conv_w, conv_b):
    f_l = ffn_up(h_l, w_up, conv_w, conv_b, SEQ, tm=1024)
    f_c = ffn_up(h_c, w_up, conv_w, conv_b, CTX_LEN, tm=CTX_LEN) if h_c is not None else None
    return f_l, f_c


def kernel(x, c, ctx, c_ctx, hgrn_lb_f, hgrn_lb_b, l0_w_mod, l0_b_mod, l0_w_in, l0_gla_wg_f, l0_gla_bg_f, l0_gla_wg_b, l0_gla_bg_b, l0_da_lq1, l0_da_lk1, l0_da_lq2, l0_da_lk2, l0_da_norm, l0_gla_norm, l0_w_out, l0_ln1_g, l0_ln1_b, l0_ffn_up, l0_ffn_conv, l0_ffn_conv_b, l0_ffn_down, l0_ln2_g, l0_ln2_b, l1_w_mod, l1_b_mod, l1_w_in, l1_hg_norm, l1_w_out, l1_ln1_g, l1_ln1_b, l1_ffn_up, l1_ffn_conv, l1_ffn_conv_b, l1_ffn_down, l1_ln2_g, l1_ln2_b):
    xl = x.reshape(BATCH * SEQ, D_MODEL)
    xc = ctx.reshape(BATCH * CTX_LEN, D_MODEL)
    cond = jnp.concatenate([c, c_ctx[None, :], jnp.zeros((MOD_ROWS - BATCH - 1, D_MODEL), F32)], axis=0)
    mods0 = modulation_table(cond, l0_w_mod, l0_b_mod)
    mods1 = modulation_table(cond, l1_w_mod, l1_b_mod)
    bf = lambda w: w.astype(BF16)
    down = lambda w: jnp.pad(bf(w), ((0, D_FF_PAD - D_FF), (0, 0)))
    tm_l, tm_c = 512, 256

    w_in = bf(l0_w_in)
    h_l = modulate(xl, mods0, _lat_group)
    h_c = modulate(xc, mods0, _ctx_group)
    q_cols = DA_HEADS * 2 * DA_DK
    pa_l = project(h_l, w_in, 0, ATTN_COLS, BF16, tm_l, q_cols=q_cols, rope=_rope_tables(), rope_cols=2 * q_cols)
    pa_c = project(h_c, w_in, 0, ATTN_COLS, BF16, tm_c, q_cols=q_cols)
    pb_l = project(h_l, w_in, ATTN_COLS, GLA_COLS, BF16, tm_l)
    pb_c = project(h_c, w_in, ATTN_COLS, GLA_COLS, BF16, tm_c)
    w_r = jnp.pad(w_in[:, ATTN_COLS + GLA_COLS:], ((0, 0), (0, LANES - 2 * GLA_GATE_RANK)))
    pr_l = project(h_l, w_r, 0, LANES, F32, tm_l, tn=LANES)
    pr_c = project(h_c, w_r, 0, LANES, F32, tm_c, tn=LANES)

    lam_vecs = (l0_da_lq1, l0_da_lk1, l0_da_lq2, l0_da_lk2)
    oa_l = diff_attention(pa_l, [(pa_l, SEQ), (pa_c, CTX_LEN)], lam_vecs, l0_da_norm, SEQ, tq=256)
    oa_c = diff_attention(pa_c, [(pa_c, CTX_LEN)], lam_vecs, l0_da_norm, CTX_LEN, tq=CTX_LEN)

    def gate_weight(wg, row0):
        w = bf(wg).reshape(GLA_GATE_RANK, GLA_HEADS, GLA_DK).transpose(1, 0, 2)
        return jnp.pad(w, ((0, 0), (row0, LANES - GLA_GATE_RANK - row0), (0, 0)))

    gla_extras = (gate_weight(l0_gla_wg_f, 0), gate_weight(l0_gla_wg_b, GLA_GATE_RANK),
                  l0_gla_bg_f.reshape(GLA_HEADS, 1, GLA_DK), l0_gla_bg_b.reshape(GLA_HEADS, 1, GLA_DK), l0_gla_norm)
    ob_l, ob_c = bidir_scan("gla", (pb_l, pr_l), (pb_c, pr_c), gla_extras, emit_ctx=True)

    w_out = bf(l0_w_out)
    x1_l, hf_l = matmul_layernorm(jnp.concatenate([oa_l, ob_l], axis=1), w_out, xl, (mods0, 2), l0_ln1_g, l0_ln1_b,
                                  (mods0, 3, 4), _lat_group, tm=tm_l)
    x1_c, hf_c = matmul_layernorm(jnp.concatenate([oa_c, ob_c], axis=1), w_out, xc, (mods0, 2), l0_ln1_g, l0_ln1_b,
                                  (mods0, 3, 4), _ctx_group, tm=tm_c)
    f_l, f_c = _ffn(hf_l, hf_c, bf(l0_ffn_up), l0_ffn_conv, l0_ffn_conv_b)
    w_down = down(l0_ffn_down)
    x2_l, h1_l = matmul_layernorm(f_l, w_down, x1_l, (mods0, 5), l0_ln2_g, l0_ln2_b, (mods1, 0, 1), _lat_group, tm=tm_l)
    _, h1_c = matmul_layernorm(f_c, w_down, x1_c, (mods0, 5), l0_ln2_g, l0_ln2_b, (mods1, 0, 1), _ctx_group, tm=tm_c)

    w_in1 = bf(l1_w_in)

    def project1(h, tm):
        return (project(h, w_in1, 0, 2 * D_MODEL, BF16, tm), project(h, w_in1, 2 * D_MODEL, 2 * D_MODEL, F32, tm),
                project(h, w_in1, 4 * D_MODEL, D_MODEL, BF16, tm))

    o1_l, _ = bidir_scan("hgrn", project1(h1_l, tm_l), project1(h1_c, tm_c), (hgrn_lb_f, hgrn_lb_b, l1_hg_norm),
                         emit_ctx=False)
    x3_l, hf1_l = matmul_layernorm(o1_l, bf(l1_w_out), x2_l, (mods1, 2), l1_ln1_g, l1_ln1_b, (mods1, 3, 4),
                                   _lat_group, tm=tm_l)
    f1_l, _ = _ffn(hf1_l, None, bf(l1_ffn_up), l1_ffn_conv, l1_ffn_conv_b)
    x4_l, _ = matmul_layernorm(f1_l, down(l1_ffn_down), x3_l, (mods1, 5), l1_ln2_g, l1_ln2_b, None, _lat_group, tm=tm_l)
    return x4_l.reshape(BATCH, SEQ, D_MODEL)
```

```python
import functools
import math

import jax
import jax.numpy as jnp
from jax import lax
from jax.experimental import pallas as pl
from jax.experimental.pallas import tpu as pltpu

F32, BF16 = jnp.float32, jnp.bfloat16

D_MODEL = 4096
BATCH = 2
SEQ = 4096
DEPTH = 2
CTX_LEN = 256
GRID_W = 64
DA_HEADS = 8
DA_DK = 128
DA_DV = 2 * DA_DK
GLA_HEADS = 8
GLA_DK = 128
GLA_DV = 256
GLA_GATE_RANK = 16
GLA_GATE_NORM = 16.0
HG_DK = 128
HG_HEADS = D_MODEL // HG_DK
HG_DV = D_MODEL // HG_HEADS
CHUNK = 64
ROPE_THETA = 10000.0
ROPE_AXIS_DIM = DA_DK // 2
D_FF = ((8 * D_MODEL // 3 + 255) // 256) * 256
CONV_W = 3
ALPHA = (2 * DEPTH) ** 0.25
LN_EPS = 1e-5
RMS_EPS = 1e-5
LAM_INIT = 0.8 - 0.6 * math.exp(-0.3 * 0)

ATTN_COLS = DA_HEADS * (2 * DA_DK + 2 * DA_DK + DA_DV)
GLA_COLS = GLA_HEADS * (2 * GLA_DK + 2 * GLA_DV)
LANES = 128
SUBLANES = 8
BF16_ROWS = 16
HALO = BF16_ROWS
MOD_ROWS = 8
CTX_GROUP = BATCH
VMEM_LIMIT = 56 * 1024 * 1024
FFN_TN = 256
PROJ_TN = 512
LN_GROUPS = 4
DOWN_TK = 512
SCAN_UNROLL = 4
D_FF_PAD = -(-D_FF // DOWN_TK) * DOWN_TK
Q_PRESCALE = DA_DK ** -0.5 * math.log2(math.e)


def _params(sem, vmem=VMEM_LIMIT):
    return pltpu.CompilerParams(dimension_semantics=sem, vmem_limit_bytes=vmem)


def _silu(x):
    return x * jax.nn.sigmoid(x)


def _log_sigmoid(x):
    return -(jnp.maximum(-x, 0.0) + jnp.log1p(jnp.exp(-jnp.abs(x))))


def _mod_spec(which, group_fn):
    return pl.BlockSpec((None, None, 1, D_MODEL), lambda i, *_: (group_fn(i), which, 0, 0))


def _vec_spec():
    return pl.BlockSpec((1, D_MODEL), lambda i, *_: (0, 0))


def _mod_kernel(c_ref, w_ref, b_ref, o_ref):
    s = _silu(c_ref[...]).astype(BF16)
    o_ref[...] = jnp.dot(s, w_ref[...].astype(BF16), preferred_element_type=F32) + b_ref[...]


def modulation_table(cond, w_mod, b_mod, tn=512):
    n = w_mod.shape[1]
    out = pl.pallas_call(
        _mod_kernel,
        out_shape=jax.ShapeDtypeStruct((MOD_ROWS, n), F32),
        grid=(n // tn,),
        in_specs=[pl.BlockSpec((MOD_ROWS, D_MODEL), lambda j: (0, 0)),
                  pl.BlockSpec((D_MODEL, tn), lambda j: (0, j)),
                  pl.BlockSpec((1, tn), lambda j: (0, j))],
        out_specs=pl.BlockSpec((MOD_ROWS, tn), lambda j: (0, j)),
        compiler_params=_params(("arbitrary",)),
        name="modulation",
    )(cond, w_mod, b_mod.reshape(1, n))
    return out.reshape(MOD_ROWS, 6, 1, D_MODEL)


def _modulate_kernel(x_ref, sh_ref, sc_ref, o_ref):
    o_ref[...] = (x_ref[...] * (1.0 + sc_ref[...]) + sh_ref[...]).astype(o_ref.dtype)


def modulate(x, mods, group_fn, tm=256):
    m = x.shape[0]
    return pl.pallas_call(
        _modulate_kernel,
        out_shape=jax.ShapeDtypeStruct((m, D_MODEL), BF16),
        grid=(m // tm,),
        in_specs=[pl.BlockSpec((tm, D_MODEL), lambda i: (i, 0)),
                  _mod_spec(0, lambda i: group_fn(i, tm)), _mod_spec(1, lambda i: group_fn(i, tm))],
        out_specs=pl.BlockSpec((tm, D_MODEL), lambda i: (i, 0)),
        compiler_params=_params(("arbitrary",)),
        name="modulate",
    )(x, mods, mods)


def _proj_kernel(*refs, n_q_tiles, n_rope_tiles):
    if n_rope_tiles:
        a_ref, w_ref, cos_ref, sin_ref, o_ref = refs
    else:
        a_ref, w_ref, o_ref = refs
    acc = jnp.dot(a_ref[...], w_ref[...], preferred_element_type=F32)
    j = pl.program_id(1)
    tm, tn = acc.shape
    if n_q_tiles:
        acc = acc * jnp.where(j < n_q_tiles, Q_PRESCALE, 1.0)
    if not n_rope_tiles:
        o_ref[...] = acc.astype(o_ref.dtype)
        return

    @pl.when(j < n_rope_tiles)
    def _():
        cos, sin = cos_ref[...], sin_ref[...]
        lane = lax.broadcasted_iota(jnp.int32, (tm, LANES), 1)
        first_half = (lane % (2 * (ROPE_AXIS_DIM // 2))) < (ROPE_AXIS_DIM // 2)
        for c in range(tn // LANES):
            xs = acc[:, c * LANES:(c + 1) * LANES]
            partner = jnp.where(first_half, pltpu.roll(xs, LANES - ROPE_AXIS_DIM // 2, 1),
                                pltpu.roll(xs, ROPE_AXIS_DIM // 2, 1))
            o_ref[:, c * LANES:(c + 1) * LANES] = (xs * cos + partner * sin).astype(o_ref.dtype)

    @pl.when(j >= n_rope_tiles)
    def _():
        o_ref[...] = acc.astype(o_ref.dtype)


def _block_cols(w, tn):
    k, n = w.shape
    return w.astype(BF16).reshape(k, n // tn, tn).transpose(1, 0, 2)


def project(a, wb, col0, n, out_dtype, tm, q_cols=0, rope=None, rope_cols=0):
    m, k = a.shape
    tn = wb.shape[2]
    assert m % tm == 0 and n % tn == 0 and col0 % tn == 0 and q_cols % tn == 0
    in_specs = [pl.BlockSpec((tm, k), lambda i, j: (i, 0)),
                pl.BlockSpec((None, k, tn), lambda i, j: (col0 // tn + j, 0, 0))]
    args = [a, wb]
    if rope is not None:
        assert rope_cols % tn == 0 and SEQ % tm == 0
        tab = pl.BlockSpec((tm, LANES), lambda i, j: (i % (SEQ // tm), 0))
        in_specs += [tab, tab]
        args += list(rope)
    return pl.pallas_call(
        functools.partial(_proj_kernel, n_q_tiles=q_cols // tn, n_rope_tiles=rope_cols // tn),
        out_shape=jax.ShapeDtypeStruct((m, n), out_dtype),
        grid=(m // tm, n // tn),
        in_specs=in_specs,
        out_specs=pl.BlockSpec((tm, tn), lambda i, j: (i, j)),
        compiler_params=_params(("arbitrary", "arbitrary")),
        name="project",
    )(*args)


def _attn_kernel(*refs, n_src):
    lq1, lk1, lq2, lk2, norm_ref, q_ref = refs[:6]
    k_refs = refs[6:6 + n_src]
    v_refs = refs[6 + n_src:6 + 2 * n_src]
    o_ref = refs[6 + 2 * n_src]
    lam = (jnp.exp(jnp.sum(lq1[...] * lk1[...], axis=-1, keepdims=True))
           - jnp.exp(jnp.sum(lq2[...] * lk2[...], axis=-1, keepdims=True)) + LAM_INIT)
    probs = []
    for mp in range(2):
        qm = q_ref[:, mp * DA_DK:(mp + 1) * DA_DK]
        ss = [lax.dot_general(qm, kr[:, mp * DA_DK:(mp + 1) * DA_DK], (((1,), (1,)), ((), ())),
                              preferred_element_type=F32) for kr in k_refs]
        mx = ss[0].max(axis=-1, keepdims=True)
        for s in ss[1:]:
            mx = jnp.maximum(mx, s.max(axis=-1, keepdims=True))
        ps = [jnp.exp2(s - mx) for s in ss]
        den = ps[0].sum(axis=-1, keepdims=True)
        for p in ps[1:]:
            den = den + p.sum(axis=-1, keepdims=True)
        probs.append((ps, 1.0 / den))
    (p1, inv1), (p2, inv2) = probs
    w2 = lam * inv2
    o = None
    for s in range(n_src):
        a = (p1[s] * inv1 - p2[s] * w2).astype(BF16)
        part = jnp.dot(a, v_refs[s][...], preferred_element_type=F32)
        o = part if o is None else o + part
    ms = jnp.mean(o * o, axis=-1, keepdims=True)
    o_ref[...] = (o * lax.rsqrt(ms + RMS_EPS) * norm_ref[...] * (1.0 - LAM_INIT)).astype(o_ref.dtype)


def diff_attention(q_src, kv_srcs, lam_vecs, da_norm, t_q, tq):
    nq = t_q // tq
    hq = 2 * DA_DK
    vec = pl.BlockSpec((1, DA_DK), lambda b, h, i: (0, 0))
    in_specs = [vec, vec, vec, vec, pl.BlockSpec((1, DA_DV), lambda b, h, i: (0, 0)),
                pl.BlockSpec((tq, hq), lambda b, h, i: (b * nq + i, h))]
    args = [v.reshape(1, DA_DK) for v in lam_vecs] + [da_norm.reshape(1, DA_DV), q_src]
    for off in (DA_HEADS, 2 * DA_HEADS):
        for arr, rows in kv_srcs:
            in_specs.append(pl.BlockSpec((rows, hq), lambda b, h, i, off=off: (b, off + h)))
            args.append(arr)
    return pl.pallas_call(
        functools.partial(_attn_kernel, n_src=len(kv_srcs)),
        out_shape=jax.ShapeDtypeStruct((BATCH * t_q, DA_HEADS * DA_DV), BF16),
        grid=(BATCH, DA_HEADS, nq),
        in_specs=in_specs,
        out_specs=pl.BlockSpec((tq, DA_DV), lambda b, h, i: (b * nq + i, h)),
        compiler_params=_params(("arbitrary", "arbitrary", "arbitrary")),
        name="diff_attention",
    )(*args)


def _chunk_steps(jobs, st_refs, emit):
    nt = (((1,), (1,)), ((), ()))
    tn = (((0,), (0,)), ((), ()))
    for jb in jobs:
        g = jb["g"]
        g_hi = g.astype(BF16)
        r1 = g - g_hi.astype(F32)
        g_mid = r1.astype(BF16)
        g_lo = (r1 - g_mid.astype(F32)).astype(BF16)
        jb["cs"] = jnp.dot(jb["tri"], jnp.concatenate([g_hi, g_mid, g_lo], axis=1), preferred_element_type=F32)
    for jb in jobs:
        q, k, cs = jb["q"], jb["k"], jb["cs"]
        dk = q.shape[-1]
        bc = cs[:, :dk] + cs[:, dk:2 * dk] + cs[:, 2 * dk:]
        b_last = bc[jb["last"]:jb["last"] + 1, :]
        jb["decay"] = jnp.exp(b_last)
        k_carry = (k * jnp.exp(b_last - bc)).astype(BF16)
        jb["upd"] = lax.dot_general(jb["vb"], k_carry, tn, preferred_element_type=F32)
        if emit:
            b_mid = bc[jb["mid"]:jb["mid"] + 1, :]
            qs = (q * jnp.exp(bc - b_mid)).astype(BF16)
            ks = (k * jnp.exp(b_mid - bc)).astype(BF16)
            jb["sc"] = lax.dot_general(qs, ks, nt, preferred_element_type=F32)
            jb["qi"] = (q * jnp.exp(bc)).astype(BF16)
    st = {}
    for jb in jobs:
        ch = jb["chain"]
        if ch not in st:
            st[ch] = st_refs[ch][...]
        jb["st_in"] = st[ch]
        st[ch] = jb["decay"] * st[ch] + jb["upd"]
    for ch, val in st.items():
        st_refs[ch][...] = val
    if not emit:
        return None
    outs = []
    for jb in jobs:
        sc = jnp.where(jb["keep"], jb["sc"], 0.0).astype(BF16)
        outs.append(jnp.dot(sc, jb["vb"], preferred_element_type=F32)
                    + lax.dot_general(jb["qi"], jb["st_in"].astype(BF16), nt, preferred_element_type=F32))
    return outs


def _readout(o, gate, norm):
    ms = jnp.mean(o * o, axis=-1, keepdims=True)
    return o * lax.rsqrt(ms + RMS_EPS) * norm * _silu(gate)


def _scan_kernel(*refs, mode, hp, dk, dv, n_lat, n_ctx, emit_ctx):
    n_stream = 5
    lat, ctx = refs[:n_stream], refs[n_stream:2 * n_stream]
    pos = 2 * n_stream
    if mode == "gla":
        wgf_ref, wgb_ref, bgf_ref, bgb_ref, norm_ref = refs[pos:pos + 5]
        pos += 5
    else:
        lbf_ref, lbb_ref, norm_ref = refs[pos:pos + 3]
        pos += 3
    out_lat = refs[pos]
    pos += 1
    out_ctx = None
    if emit_ctx:
        out_ctx = refs[pos]
        pos += 1
    st_refs = refs[pos:pos + 2 * hp]
    pos += 2 * hp
    acc_lat = refs[pos]
    acc_ctx = refs[pos + 1] if emit_ctx else None

    row = lax.broadcasted_iota(jnp.int32, (CHUNK, CHUNK), 0)
    col = lax.broadcasted_iota(jnp.int32, (CHUNK, CHUNK), 1)
    keep_f, keep_b = col <= row, col >= row
    tri_f = jnp.where(keep_f, 1.0, 0.0).astype(BF16)
    tri_b = jnp.where(keep_b, 1.0, 0.0).astype(BF16)
    dirs = (("f", tri_f, keep_f, CHUNK // 2 - 1, CHUNK - 1), ("b", tri_b, keep_b, CHUNK // 2, 0))

    if mode == "hgrn":
        def lower_bound(tab_ref):
            t = tab_ref[...]
            e = jnp.exp(t - jnp.max(t, axis=0, keepdims=True))
            s = e / jnp.sum(e, axis=0, keepdims=True)
            return (s[0:1] + s[1:2]) - s[0:1]
        lbs = {"f": lower_bound(lbf_ref), "b": lower_bound(lbb_ref)}

    def load(stream, rows, d, h):
        kc = slice(h * dk, (h + 1) * dk)
        vc = slice(h * dv, (h + 1) * dv)
        if mode == "gla":
            q_ref, k_ref, v_ref, _, r_ref = stream
            wg_ref, bg_ref = (wgf_ref, bgf_ref) if d == "f" else (wgb_ref, bgb_ref)
            q = q_ref[rows, kc].astype(F32) * (dk ** -0.5)
            k = k_ref[rows, kc].astype(F32)
            pre = jnp.dot(r_ref[rows, :].astype(BF16), wg_ref[h], preferred_element_type=F32) + bg_ref[h]
            g = _log_sigmoid(pre) * (1.0 / GLA_GATE_NORM)
        else:
            q_ref, v_ref, zf_ref, zb_ref, _ = stream
            z_ref = zf_ref if d == "f" else zb_ref
            q = _silu(q_ref[rows, kc].astype(F32)) * (dk ** -0.5)
            lb = lbs[d][:, kc]
            f = lb + (1.0 - lb) * jax.nn.sigmoid(z_ref[rows, kc])
            k = 1.0 - f
            g = jnp.log(f)
        return q, k, v_ref[rows, vc], g

    def gate_of(stream):
        return stream[3] if mode == "gla" else stream[4]

    def run(stream, n_chunks, emit, acc_ref, out_ref):
        half = n_chunks // 2
        per_step = min(SCAN_UNROLL, half)

        def make_body(final):
            def body(c, carry):
                jobs = []
                for u in range(per_step):
                    step = c * per_step + u
                    for di, (d, tri, keep, mid, last) in enumerate(dirs):
                        cc = step if d == "f" else n_chunks - 1 - step
                        rows = pl.ds(pl.multiple_of(cc * CHUNK, CHUNK), CHUNK)
                        for h in range(hp):
                            q, k, vb, g = load(stream, rows, d, h)
                            jobs.append(dict(q=q, k=k, vb=vb, g=g, tri=tri, keep=keep, mid=mid, last=last,
                                             chain=di * hp + h, rows=rows, vc=slice(h * dv, (h + 1) * dv)))
                outs = _chunk_steps(jobs, st_refs, emit)
                if emit:
                    for jb, o in zip(jobs, outs):
                        rows, vc = jb["rows"], jb["vc"]
                        if final:
                            gate = gate_of(stream)[rows, vc].astype(F32)
                            out_ref[rows, vc] = _readout(acc_ref[rows, vc] + o, gate,
                                                         norm_ref[...]).astype(out_ref.dtype)
                        else:
                            acc_ref[rows, vc] = o
                return carry
            return body

        lax.fori_loop(0, half // per_step, make_body(False), 0)
        lax.fori_loop(half // per_step, n_chunks // per_step, make_body(True), 0)

    for st_ref in st_refs:
        st_ref[...] = jnp.zeros(st_ref.shape, F32)
    run(ctx, n_ctx, emit_ctx, acc_ctx, out_ctx)
    run(lat, n_lat, True, acc_lat, out_lat)


def bidir_scan(mode, proj_lat, proj_ctx, extras, emit_ctx, hp=2):
    if mode == "gla":
        heads, dk, dv = GLA_HEADS, GLA_DK, GLA_DV
        nb = heads // hp

        def stream(proj, rows):
            main, r = proj
            specs = [pl.BlockSpec((rows, hp * dk), lambda b, h: (b, h)),
                     pl.BlockSpec((rows, hp * dk), lambda b, h: (b, nb + h)),
                     pl.BlockSpec((rows, hp * dv), lambda b, h: (b, nb + h)),
                     pl.BlockSpec((rows, hp * dv), lambda b, h: (b, 2 * nb + h)),
                     pl.BlockSpec((rows, LANES), lambda b, h: (b, 0))]
            return specs, [main, main, main, main, r]

        wgf, wgb, bgf, bgb, norm = extras
        sq = pl.BlockSpec((hp, LANES, dk), lambda b, h: (h, 0, 0))
        bias = pl.BlockSpec((hp, 1, dk), lambda b, h: (h, 0, 0))
        extra_specs = [sq, sq, bias, bias, pl.BlockSpec((1, dv), lambda b, h: (0, 0))]
        extra_args = [wgf, wgb, bgf, bgb, norm.reshape(1, dv)]
    else:
        heads, dk, dv = HG_HEADS, HG_DK, HG_DV
        nb = heads // hp

        def stream(proj, rows):
            pq, pz, pg = proj
            blk = lambda off: pl.BlockSpec((rows, hp * dk), lambda b, h: (b, off + h))
            return [blk(0), blk(nb), blk(0), blk(nb), blk(0)], [pq, pq, pz, pz, pg]

        lbf, lbb, norm = extras
        tab = pl.BlockSpec((DEPTH, hp * dk), lambda b, h: (0, h))
        extra_specs = [tab, tab, pl.BlockSpec((1, dv), lambda b, h: (0, 0))]
        extra_args = [lbf, lbb, norm.reshape(1, dv)]

    ls, la = stream(proj_lat, SEQ)
    cs, ca = stream(proj_ctx, CTX_LEN)
    out_shape = [jax.ShapeDtypeStruct((BATCH * SEQ, heads * dv), BF16)]
    out_specs = [pl.BlockSpec((SEQ, hp * dv), lambda b, h: (b, h))]
    scratch = [pltpu.VMEM((dv, dk), F32) for _ in range(2 * hp)] + [pltpu.VMEM((SEQ, hp * dv), F32)]
    if emit_ctx:
        out_shape.append(jax.ShapeDtypeStruct((BATCH * CTX_LEN, heads * dv), BF16))
        out_specs.append(pl.BlockSpec((CTX_LEN, hp * dv), lambda b, h: (b, h)))
        scratch.append(pltpu.VMEM((CTX_LEN, hp * dv), F32))
    outs = pl.pallas_call(
        functools.partial(_scan_kernel, mode=mode, hp=hp, dk=dk, dv=dv, n_lat=SEQ // CHUNK,
                          n_ctx=CTX_LEN // CHUNK, emit_ctx=emit_ctx),
        out_shape=out_shape,
        grid=(BATCH, nb),
        in_specs=ls + cs + extra_specs,
        out_specs=out_specs,
        scratch_shapes=scratch,
        compiler_params=_params(("arbitrary", "arbitrary")),
        name="bidir_scan_" + mode,
    )(*la, *ca, *extra_args)
    return outs if emit_ctx else (outs[0], None)


def _mm_ln_kernel(*refs, nk, with_h):
    if with_h:
        a_ref, w_ref, x_ref, gate_ref, g_ref, b_ref, sh_ref, sc_ref, xo_ref, ho_ref, hst_ref = refs
    else:
        a_ref, w_ref, x_ref, gate_ref, g_ref, b_ref, xo_ref = refs
    kk = pl.program_id(1)
    tn = 512

    def accumulate(first):
        a = a_ref[...]
        for n in range(D_MODEL // tn):
            cols = slice(n * tn, (n + 1) * tn)
            part = jnp.dot(a, w_ref[:, cols], preferred_element_type=F32)
            if first:
                xo_ref[:, cols] = part
            else:
                xo_ref[:, cols] += part

    pl.when(kk == 0)(functools.partial(accumulate, True))
    pl.when(kk > 0)(functools.partial(accumulate, False))

    @pl.when(kk == nk - 1)
    def _():
        def body(r, carry):
            base = r * (LN_GROUPS * SUBLANES)
            rows = [pl.ds(pl.multiple_of(base + t * SUBLANES, SUBLANES), SUBLANES) for t in range(LN_GROUPS)]
            ys = [ALPHA * x_ref[rw, :] + gate_ref[...] * xo_ref[rw, :] for rw in rows]
            mus = [jnp.mean(y, axis=-1, keepdims=True) for y in ys]
            dls = [y - mu for y, mu in zip(ys, mus)]
            rss = [lax.rsqrt(jnp.mean(d * d, axis=-1, keepdims=True) + LN_EPS) for d in dls]
            zs = [d * rs * g_ref[...] + b_ref[...] for d, rs in zip(dls, rss)]
            for rw, z in zip(rows, zs):
                xo_ref[rw, :] = z
            if with_h:
                for t, z in enumerate(zs):
                    hst_ref[t * SUBLANES:(t + 1) * SUBLANES, :] = z * (1.0 + sc_ref[...]) + sh_ref[...]
                ho_ref[pl.ds(pl.multiple_of(base, LN_GROUPS * SUBLANES), LN_GROUPS * SUBLANES), :] = (
                    hst_ref[...].astype(ho_ref.dtype))
            return carry

        lax.fori_loop(0, xo_ref.shape[0] // (LN_GROUPS * SUBLANES), body, 0)


def matmul_layernorm(a, w, x, gate, ln_g, ln_b, next_mod, group_fn, tm, tk=DOWN_TK):
    m, k = a.shape
    assert m % tm == 0 and k % tk == 0
    nk = k // tk
    gfn = lambda i: group_fn(i, tm)
    in_specs = [pl.BlockSpec((tm, tk), lambda i, kk: (i, kk)),
                pl.BlockSpec((tk, D_MODEL), lambda i, kk: (kk, 0)),
                pl.BlockSpec((tm, D_MODEL), lambda i, kk: (i, 0)),
                _mod_spec(gate[1], gfn), _vec_spec(), _vec_spec()]
    args = [a, w, x, gate[0], ln_g.reshape(1, D_MODEL), ln_b.reshape(1, D_MODEL)]
    out_shape = [jax.ShapeDtypeStruct((m, D_MODEL), F32)]
    out_specs = [pl.BlockSpec((tm, D_MODEL), lambda i, kk: (i, 0))]
    if next_mod is not None:
        in_specs += [_mod_spec(next_mod[1], gfn), _mod_spec(next_mod[2], gfn)]
        args += [next_mod[0], next_mod[0]]
        out_shape.append(jax.ShapeDtypeStruct((m, D_MODEL), BF16))
        out_specs.append(pl.BlockSpec((tm, D_MODEL), lambda i, kk: (i, 0)))
    outs = pl.pallas_call(
        functools.partial(_mm_ln_kernel, nk=nk, with_h=next_mod is not None),
        out_shape=out_shape,
        grid=(m // tm, nk),
        in_specs=in_specs,
        out_specs=out_specs,
        scratch_shapes=[pltpu.VMEM((LN_GROUPS * SUBLANES, D_MODEL), F32)] if next_mod is not None else [],
        compiler_params=_params(("arbitrary", "arbitrary")),
        name="matmul_layernorm",
    )(*args)
    return (outs[0], outs[1]) if next_mod is not None else (outs[0], None)


def _ffn_up_kernel(*refs, tiles_per_seq, n_tiles, per_step):
    a_ref, prev_ref, next_ref = refs[:3]
    tile_refs = [refs[3 + 6 * t:9 + 6 * t] for t in range(per_step)]
    o_ref, aext_ref = refs[3 + 6 * per_step:]
    i, j = pl.program_id(0), pl.program_id(1)
    tm = a_ref.shape[0]
    ext = tm + 2 * HALO

    @pl.when(j == 0)
    def _():
        first = (i % tiles_per_seq) == 0
        last = (i % tiles_per_seq) == tiles_per_seq - 1
        zero = jnp.zeros(prev_ref.shape, prev_ref.dtype)
        aext_ref[0:HALO, :] = jnp.where(first, zero, prev_ref[...])
        aext_ref[HALO:HALO + tm, :] = a_ref[...]
        aext_ref[HALO + tm:, :] = jnp.where(last, zero, next_ref[...])

    a = aext_ref[...]
    us = [(jnp.dot(a, wg[...], preferred_element_type=F32), jnp.dot(a, wv[...], preferred_element_type=F32))
          for wg, wv, *_ in tile_refs]

    def conv(u, cw_ref, cb_ref):
        up = pltpu.roll(u, 1, 0)[HALO:HALO + tm]
        un = pltpu.roll(u, ext - 1, 0)[HALO:HALO + tm]
        return cw_ref[0:1, :] * up + cw_ref[1:2, :] * u[HALO:HALO + tm] + cw_ref[2:3, :] * un + cb_ref[...]

    for t, ((ug, uv), (_, _, cwg, cwv, cbg, cbv)) in enumerate(zip(us, tile_refs)):
        tn = ug.shape[1]
        act = _silu(conv(ug, cwg, cbg)) * conv(uv, cwv, cbv)
        act = jnp.where(j * per_step + t < n_tiles, act, 0.0)
        o_ref[:, t * tn:(t + 1) * tn] = act.astype(o_ref.dtype)


def ffn_up(h, wb, conv_w, conv_b, seq_len, tm, per_step=2):
    m, k = h.shape
    tn = wb.shape[2]
    n_tiles = D_FF // tn
    assert seq_len % tm == 0 and D_FF % tn == 0 and D_FF_PAD % (per_step * tn) == 0 and tm % HALO == 0
    hb = tm // HALO
    last_hb = m // HALO - 1
    cb = conv_b.reshape(1, 2 * D_FF)
    in_specs = [pl.BlockSpec((tm, k), lambda i, j: (i, 0)),
                pl.BlockSpec((HALO, k), lambda i, j: (jnp.maximum(i * hb - 1, 0), 0)),
                pl.BlockSpec((HALO, k), lambda i, j: (jnp.minimum((i + 1) * hb, last_hb), 0))]
    args = [h, h, h]
    for t in range(per_step):
        tc = lambda j, t=t: jnp.minimum(j * per_step + t, n_tiles - 1)
        in_specs += [pl.BlockSpec((None, k, tn), lambda i, j, tc=tc: (tc(j), 0, 0)),
                     pl.BlockSpec((None, k, tn), lambda i, j, tc=tc: (n_tiles + tc(j), 0, 0)),
                     pl.BlockSpec((CONV_W, tn), lambda i, j, tc=tc: (0, tc(j))),
                     pl.BlockSpec((CONV_W, tn), lambda i, j, tc=tc: (0, n_tiles + tc(j))),
                     pl.BlockSpec((1, tn), lambda i, j, tc=tc: (0, tc(j))),
                     pl.BlockSpec((1, tn), lambda i, j, tc=tc: (0, n_tiles + tc(j)))]
        args += [wb, wb, conv_w, conv_w, cb, cb]
    return pl.pallas_call(
        functools.partial(_ffn_up_kernel, tiles_per_seq=seq_len // tm, n_tiles=n_tiles, per_step=per_step),
        out_shape=jax.ShapeDtypeStruct((m, D_FF_PAD), BF16),
        grid=(m // tm, D_FF_PAD // (per_step * tn)),
        in_specs=in_specs,
        out_specs=pl.BlockSpec((tm, per_step * tn), lambda i, j: (i, j)),
        scratch_shapes=[pltpu.VMEM((tm + 2 * HALO, k), BF16)],
        compiler_params=_params(("arbitrary", "arbitrary")),
        name="ffn_up",
    )(*args)


def _rope_tables():
    rows = SEQ // GRID_W
    row = jnp.repeat(jnp.arange(rows, dtype=F32), GRID_W)
    col = jnp.tile(jnp.arange(GRID_W, dtype=F32), rows)
    half = ROPE_AXIS_DIM // 2
    inv = ROPE_THETA ** (-jnp.arange(half, dtype=F32) / half)
    ang_r, ang_c = row[:, None] * inv, col[:, None] * inv
    ang = jnp.concatenate([ang_r, ang_r, ang_c, ang_c], axis=-1)
    sign = jnp.tile(jnp.concatenate([-jnp.ones((half,), F32), jnp.ones((half,), F32)]), 2)
    return jnp.cos(ang), jnp.sin(ang) * sign


def _lat_group(i, tm):
    return i // (SEQ // tm)


def _ctx_group(i, tm):
    return CTX_GROUP


def _ffn(h_l, h_c, w_up, conv_w, conv_b):
    f_l = ffn_up(h_l, w_up, conv_w, conv_b, SEQ, tm=1024)
    f_c = ffn_up(h_c, w_up, conv_w, conv_b, CTX_LEN, tm=CTX_LEN) if h_c is not None else None
    return f_l, f_c


def kernel(x, c, ctx, c_ctx, hgrn_lb_f, hgrn_lb_b, l0_w_mod, l0_b_mod, l0_w_in, l0_gla_wg_f, l0_gla_bg_f, l0_gla_wg_b, l0_gla_bg_b, l0_da_lq1, l0_da_lk1, l0_da_lq2, l0_da_lk2, l0_da_norm, l0_gla_norm, l0_w_out, l0_ln1_g, l0_ln1_b, l0_ffn_up, l0_ffn_conv, l0_ffn_conv_b, l0_ffn_down, l0_ln2_g, l0_ln2_b, l1_w_mod, l1_b_mod, l1_w_in, l1_hg_norm, l1_w_out, l1_ln1_g, l1_ln1_b, l1_ffn_up, l1_ffn_conv, l1_ffn_conv_b, l1_ffn_down, l1_ln2_g, l1_ln2_b):
    xl = x.reshape(BATCH * SEQ, D_MODEL)
    xc = ctx.reshape(BATCH * CTX_LEN, D_MODEL)
    cond = jnp.concatenate([c, c_ctx[None, :], jnp.zeros((MOD_ROWS - BATCH - 1, D_MODEL), F32)], axis=0)
    mods0 = modulation_table(cond, l0_w_mod, l0_b_mod)
    mods1 = modulation_table(cond, l1_w_mod, l1_b_mod)
    bf = lambda w: w.astype(BF16)
    down = lambda w: jnp.concatenate([bf(w), jnp.zeros((D_FF_PAD - D_FF, D_MODEL), BF16)], axis=0)
    tm_l, tm_c = 512, 256
    pm_l, pm_c = 1024, 256

    w_in = _block_cols(l0_w_in[:, :ATTN_COLS + GLA_COLS], PROJ_TN)
    h_l = modulate(xl, mods0, _lat_group)
    h_c = modulate(xc, mods0, _ctx_group)
    q_cols = DA_HEADS * 2 * DA_DK
    pa_l = project(h_l, w_in, 0, ATTN_COLS, BF16, pm_l, q_cols=q_cols, rope=_rope_tables(), rope_cols=2 * q_cols)
    pa_c = project(h_c, w_in, 0, ATTN_COLS, BF16, pm_c, q_cols=q_cols)
    pb_l = project(h_l, w_in, ATTN_COLS, GLA_COLS, BF16, pm_l)
    pb_c = project(h_c, w_in, ATTN_COLS, GLA_COLS, BF16, pm_c)
    w_r = _block_cols(jnp.pad(l0_w_in[:, ATTN_COLS + GLA_COLS:], ((0, 0), (0, LANES - 2 * GLA_GATE_RANK))), LANES)
    pr_l = project(h_l, w_r, 0, LANES, F32, pm_l)
    pr_c = project(h_c, w_r, 0, LANES, F32, pm_c)

    lam_vecs = (l0_da_lq1, l0_da_lk1, l0_da_lq2, l0_da_lk2)
    oa_l = diff_attention(pa_l, [(pa_l, SEQ), (pa_c, CTX_LEN)], lam_vecs, l0_da_norm, SEQ, tq=256)
    oa_c = diff_attention(pa_c, [(pa_c, CTX_LEN)], lam_vecs, l0_da_norm, CTX_LEN, tq=CTX_LEN)

    def gate_weight(wg, row0):
        w = bf(wg).reshape(GLA_GATE_RANK, GLA_HEADS, GLA_DK).transpose(1, 0, 2)
        return jnp.pad(w, ((0, 0), (row0, LANES - GLA_GATE_RANK - row0), (0, 0)))

    gla_extras = (gate_weight(l0_gla_wg_f, 0), gate_weight(l0_gla_wg_b, GLA_GATE_RANK),
                  l0_gla_bg_f.reshape(GLA_HEADS, 1, GLA_DK), l0_gla_bg_b.reshape(GLA_HEADS, 1, GLA_DK), l0_gla_norm)
    ob_l, ob_c = bidir_scan("gla", (pb_l, pr_l), (pb_c, pr_c), gla_extras, emit_ctx=True)

    w_out = bf(l0_w_out)
    x1_l, hf_l = matmul_layernorm(jnp.concatenate([oa_l, ob_l], axis=1), w_out, xl, (mods0, 2), l0_ln1_g, l0_ln1_b,
                                  (mods0, 3, 4), _lat_group, tm=tm_l)
    x1_c, hf_c = matmul_layernorm(jnp.concatenate([oa_c, ob_c], axis=1), w_out, xc, (mods0, 2), l0_ln1_g, l0_ln1_b,
                                  (mods0, 3, 4), _ctx_group, tm=tm_c)
    f_l, f_c = _ffn(hf_l, hf_c, _block_cols(l0_ffn_up, FFN_TN), l0_ffn_conv, l0_ffn_conv_b)
    w_down = down(l0_ffn_down)
    x2_l, h1_l = matmul_layernorm(f_l, w_down, x1_l, (mods0, 5), l0_ln2_g, l0_ln2_b, (mods1, 0, 1), _lat_group, tm=tm_l)
    _, h1_c = matmul_layernorm(f_c, w_down, x1_c, (mods0, 5), l0_ln2_g, l0_ln2_b, (mods1, 0, 1), _ctx_group, tm=tm_c)

    w_in1 = _block_cols(l1_w_in, PROJ_TN)

    def project1(h, tm):
        return (project(h, w_in1, 0, 2 * D_MODEL, BF16, tm), project(h, w_in1, 2 * D_MODEL, 2 * D_MODEL, F32, tm),
                project(h, w_in1, 4 * D_MODEL, D_MODEL, BF16, tm))

    o1_l, _ = bidir_scan("hgrn", project1(h1_l, pm_l), project1(h1_c, pm_c), (hgrn_lb_f, hgrn_lb_b, l1_hg_norm),
                         emit_ctx=False)
    x3_l, hf1_l = matmul_layernorm(o1_l, bf(l1_w_out), x2_l, (mods1, 2), l1_ln1_g, l1_ln1_b, (mods1, 3, 4),
                                   _lat_group, tm=tm_l)
    f1_l, _ = _ffn(hf1_l, None, _block_cols(l1_ffn_up, FFN_TN), l1_ffn_conv, l1_ffn_conv_b)
    x4_l, _ = matmul_layernorm(f1_l, down(l1_ffn_down), x3_l, (mods1, 5), l1_ln2_g, l1_ln2_b, None, _lat_group, tm=tm_l)
    return x4_l.reshape(BATCH, SEQ, D_MODEL)
```

```python
import functools
import math

import jax
import jax.numpy as jnp
from jax import lax
from jax.experimental import pallas as pl
from jax.experimental.pallas import tpu as pltpu

F32, BF16 = jnp.float32, jnp.bfloat16

D_MODEL = 4096
BATCH = 2
SEQ = 4096
DEPTH = 2
CTX_LEN = 256
GRID_W = 64
DA_HEADS = 8
DA_DK = 128
DA_DV = 2 * DA_DK
GLA_HEADS = 8
GLA_DK = 128
GLA_DV = 256
GLA_GATE_RANK = 16
GLA_GATE_NORM = 16.0
HG_DK = 128
HG_HEADS = D_MODEL // HG_DK
HG_DV = D_MODEL // HG_HEADS
CHUNK = 64
ROPE_THETA = 10000.0
ROPE_AXIS_DIM = DA_DK // 2
D_FF = ((8 * D_MODEL // 3 + 255) // 256) * 256
CONV_W = 3
ALPHA = (2 * DEPTH) ** 0.25
LN_EPS = 1e-5
RMS_EPS = 1e-5
LAM_INIT = 0.8 - 0.6 * math.exp(-0.3 * 0)

ATTN_COLS = DA_HEADS * (2 * DA_DK + 2 * DA_DK + DA_DV)
GLA_COLS = GLA_HEADS * (2 * GLA_DK + 2 * GLA_DV)
LANES = 128
SUBLANES = 8
BF16_ROWS = 16
HALO = BF16_ROWS
MOD_ROWS = 8
CTX_GROUP = BATCH
VMEM_LIMIT = 56 * 1024 * 1024
PROJ_VMEM_LIMIT = 58 * 1024 * 1024
FFN_TN = 256
PROJ_TN = 1024
LN_GROUPS = 4
DOWN_TK = 512
SCAN_UNROLL = 4
D_FF_PAD = -(-D_FF // DOWN_TK) * DOWN_TK
Q_PRESCALE = DA_DK ** -0.5 * math.log2(math.e)


def _params(sem, vmem=VMEM_LIMIT):
    return pltpu.CompilerParams(dimension_semantics=sem, vmem_limit_bytes=vmem)


def _silu(x):
    return x * jax.nn.sigmoid(x)


def _log_sigmoid(x):
    return -(jnp.maximum(-x, 0.0) + jnp.log1p(jnp.exp(-jnp.abs(x))))


def _mod_spec(which, group_fn):
    return pl.BlockSpec((None, None, 1, D_MODEL), lambda i, *_: (group_fn(i), which, 0, 0))


def _vec_spec():
    return pl.BlockSpec((1, D_MODEL), lambda i, *_: (0, 0))


def _mod_kernel(c_ref, w_ref, b_ref, o_ref):
    s = _silu(c_ref[...]).astype(BF16)
    o_ref[...] = jnp.dot(s, w_ref[...].astype(BF16), preferred_element_type=F32) + b_ref[...]


def modulation_table(cond, w_mod, b_mod, tn=512):
    n = w_mod.shape[1]
    out = pl.pallas_call(
        _mod_kernel,
        out_shape=jax.ShapeDtypeStruct((MOD_ROWS, n), F32),
        grid=(n // tn,),
        in_specs=[pl.BlockSpec((MOD_ROWS, D_MODEL), lambda j: (0, 0)),
                  pl.BlockSpec((D_MODEL, tn), lambda j: (0, j)),
                  pl.BlockSpec((1, tn), lambda j: (0, j))],
        out_specs=pl.BlockSpec((MOD_ROWS, tn), lambda j: (0, j)),
        compiler_params=_params(("arbitrary",)),
        name="modulation",
    )(cond, w_mod, b_mod.reshape(1, n))
    return out.reshape(MOD_ROWS, 6, 1, D_MODEL)


def _modulate_kernel(x_ref, sh_ref, sc_ref, o_ref):
    o_ref[...] = (x_ref[...] * (1.0 + sc_ref[...]) + sh_ref[...]).astype(o_ref.dtype)


def modulate(x, mods, group_fn, tm=256):
    m = x.shape[0]
    return pl.pallas_call(
        _modulate_kernel,
        out_shape=jax.ShapeDtypeStruct((m, D_MODEL), BF16),
        grid=(m // tm,),
        in_specs=[pl.BlockSpec((tm, D_MODEL), lambda i: (i, 0)),
                  _mod_spec(0, lambda i: group_fn(i, tm)), _mod_spec(1, lambda i: group_fn(i, tm))],
        out_specs=pl.BlockSpec((tm, D_MODEL), lambda i: (i, 0)),
        compiler_params=_params(("arbitrary",)),
        name="modulate",
    )(x, mods, mods)


def _proj_kernel(*refs, n_q_tiles, n_rope_tiles):
    if n_rope_tiles:
        a_ref, w_ref, cos_ref, sin_ref, o_ref, wbf_ref = refs
    else:
        a_ref, w_ref, o_ref, wbf_ref = refs
    j, i = pl.program_id(0), pl.program_id(1)

    @pl.when(i == 0)
    def _():
        wbf_ref[...] = w_ref[...].astype(BF16)

    acc = jnp.dot(a_ref[...], wbf_ref[...], preferred_element_type=F32)
    tm, tn = acc.shape
    if n_q_tiles:
        acc = acc * jnp.where(j < n_q_tiles, Q_PRESCALE, 1.0)
    if not n_rope_tiles:
        o_ref[...] = acc.astype(o_ref.dtype)
        return

    @pl.when(j < n_rope_tiles)
    def _():
        cos, sin = cos_ref[...], sin_ref[...]
        lane = lax.broadcasted_iota(jnp.int32, (tm, LANES), 1)
        first_half = (lane % (2 * (ROPE_AXIS_DIM // 2))) < (ROPE_AXIS_DIM // 2)
        for c in range(tn // LANES):
            xs = acc[:, c * LANES:(c + 1) * LANES]
            partner = jnp.where(first_half, pltpu.roll(xs, LANES - ROPE_AXIS_DIM // 2, 1),
                                pltpu.roll(xs, ROPE_AXIS_DIM // 2, 1))
            o_ref[:, c * LANES:(c + 1) * LANES] = (xs * cos + partner * sin).astype(o_ref.dtype)

    @pl.when(j >= n_rope_tiles)
    def _():
        o_ref[...] = acc.astype(o_ref.dtype)


def project(a, w, col0, n, out_dtype, tm, tn=PROJ_TN, q_cols=0, rope=None, rope_cols=0):
    m, k = a.shape
    assert m % tm == 0 and n % tn == 0 and col0 % tn == 0 and q_cols % tn == 0
    in_specs = [pl.BlockSpec((tm, k), lambda j, i: (i, 0)),
                pl.BlockSpec((k, tn), lambda j, i: (0, col0 // tn + j))]
    args = [a, w]
    if rope is not None:
        assert rope_cols % tn == 0 and SEQ % tm == 0
        tab = pl.BlockSpec((tm, LANES), lambda j, i: (i % (SEQ // tm), 0))
        in_specs += [tab, tab]
        args += list(rope)
    return pl.pallas_call(
        functools.partial(_proj_kernel, n_q_tiles=q_cols // tn, n_rope_tiles=rope_cols // tn),
        out_shape=jax.ShapeDtypeStruct((m, n), out_dtype),
        grid=(n // tn, m // tm),
        in_specs=in_specs,
        out_specs=pl.BlockSpec((tm, tn), lambda j, i: (i, j)),
        scratch_shapes=[pltpu.VMEM((k, tn), BF16)],
        compiler_params=_params(("arbitrary", "arbitrary"), vmem=PROJ_VMEM_LIMIT),
        name="project",
    )(*args)


def _attn_kernel(*refs, n_src):
    lq1, lk1, lq2, lk2, norm_ref, q_ref = refs[:6]
    k_refs = refs[6:6 + n_src]
    v_refs = refs[6 + n_src:6 + 2 * n_src]
    o_ref = refs[6 + 2 * n_src]
    lam = (jnp.exp(jnp.sum(lq1[...] * lk1[...], axis=-1, keepdims=True))
           - jnp.exp(jnp.sum(lq2[...] * lk2[...], axis=-1, keepdims=True)) + LAM_INIT)
    probs = []
    for mp in range(2):
        qm = q_ref[:, mp * DA_DK:(mp + 1) * DA_DK]
        ss = [lax.dot_general(qm, kr[:, mp * DA_DK:(mp + 1) * DA_DK], (((1,), (1,)), ((), ())),
                              preferred_element_type=F32) for kr in k_refs]
        mx = ss[0].max(axis=-1, keepdims=True)
        for s in ss[1:]:
            mx = jnp.maximum(mx, s.max(axis=-1, keepdims=True))
        ps = [jnp.exp2(s - mx) for s in ss]
        den = ps[0].sum(axis=-1, keepdims=True)
        for p in ps[1:]:
            den = den + p.sum(axis=-1, keepdims=True)
        probs.append((ps, 1.0 / den))
    (p1, inv1), (p2, inv2) = probs
    w2 = lam * inv2
    o = None
    for s in range(n_src):
        a = (p1[s] * inv1 - p2[s] * w2).astype(BF16)
        part = jnp.dot(a, v_refs[s][...], preferred_element_type=F32)
        o = part if o is None else o + part
    ms = jnp.mean(o * o, axis=-1, keepdims=True)
    o_ref[...] = (o * lax.rsqrt(ms + RMS_EPS) * norm_ref[...] * (1.0 - LAM_INIT)).astype(o_ref.dtype)


def diff_attention(q_src, kv_srcs, lam_vecs, da_norm, t_q, tq):
    nq = t_q // tq
    hq = 2 * DA_DK
    vec = pl.BlockSpec((1, DA_DK), lambda b, h, i: (0, 0))
    in_specs = [vec, vec, vec, vec, pl.BlockSpec((1, DA_DV), lambda b, h, i: (0, 0)),
                pl.BlockSpec((tq, hq), lambda b, h, i: (b * nq + i, h))]
    args = [v.reshape(1, DA_DK) for v in lam_vecs] + [da_norm.reshape(1, DA_DV), q_src]
    for off in (DA_HEADS, 2 * DA_HEADS):
        for arr, rows in kv_srcs:
            in_specs.append(pl.BlockSpec((rows, hq), lambda b, h, i, off=off: (b, off + h)))
            args.append(arr)
    return pl.pallas_call(
        functools.partial(_attn_kernel, n_src=len(kv_srcs)),
        out_shape=jax.ShapeDtypeStruct((BATCH * t_q, DA_HEADS * DA_DV), BF16),
        grid=(BATCH, DA_HEADS, nq),
        in_specs=in_specs,
        out_specs=pl.BlockSpec((tq, DA_DV), lambda b, h, i: (b * nq + i, h)),
        compiler_params=_params(("arbitrary", "arbitrary", "arbitrary")),
        name="diff_attention",
    )(*args)


def _chunk_steps(jobs, st_refs, emit):
    nt = (((1,), (1,)), ((), ()))
    tn = (((0,), (0,)), ((), ()))
    for jb in jobs:
        g = jb["g"]
        g_hi = g.astype(BF16)
        r1 = g - g_hi.astype(F32)
        g_mid = r1.astype(BF16)
        g_lo = (r1 - g_mid.astype(F32)).astype(BF16)
        jb["cs"] = jnp.dot(jb["tri"], jnp.concatenate([g_hi, g_mid, g_lo], axis=1), preferred_element_type=F32)
    for jb in jobs:
        q, k, cs = jb["q"], jb["k"], jb["cs"]
        dk = q.shape[-1]
        bc = cs[:, :dk] + cs[:, dk:2 * dk] + cs[:, 2 * dk:]
        b_last = bc[jb["last"]:jb["last"] + 1, :]
        jb["decay"] = jnp.exp(b_last)
        k_carry = (k * jnp.exp(b_last - bc)).astype(BF16)
        jb["upd"] = lax.dot_general(jb["vb"], k_carry, tn, preferred_element_type=F32)
        if emit:
            b_mid = bc[jb["mid"]:jb["mid"] + 1, :]
            qs = (q * jnp.exp(bc - b_mid)).astype(BF16)
            ks = (k * jnp.exp(b_mid - bc)).astype(BF16)
            jb["sc"] = lax.dot_general(qs, ks, nt, preferred_element_type=F32)
            jb["qi"] = (q * jnp.exp(bc)).astype(BF16)
    st = {}
    for jb in jobs:
        ch = jb["chain"]
        if ch not in st:
            st[ch] = st_refs[ch][...]
        jb["st_in"] = st[ch]
        st[ch] = jb["decay"] * st[ch] + jb["upd"]
    for ch, val in st.items():
        st_refs[ch][...] = val
    if not emit:
        return None
    outs = []
    for jb in jobs:
        sc = jnp.where(jb["keep"], jb["sc"], 0.0).astype(BF16)
        outs.append(jnp.dot(sc, jb["vb"], preferred_element_type=F32)
                    + lax.dot_general(jb["qi"], jb["st_in"].astype(BF16), nt, preferred_element_type=F32))
    return outs


def _readout(o, gate, norm):
    ms = jnp.mean(o * o, axis=-1, keepdims=True)
    return o * lax.rsqrt(ms + RMS_EPS) * norm * _silu(gate)


def _scan_kernel(*refs, mode, hp, dk, dv, n_lat, n_ctx, emit_ctx):
    n_stream = 5
    lat, ctx = refs[:n_stream], refs[n_stream:2 * n_stream]
    pos = 2 * n_stream
    if mode == "gla":
        wgf_ref, wgb_ref, bgf_ref, bgb_ref, norm_ref = refs[pos:pos + 5]
        pos += 5
    else:
        lbf_ref, lbb_ref, norm_ref = refs[pos:pos + 3]
        pos += 3
    out_lat = refs[pos]
    pos += 1
    out_ctx = None
    if emit_ctx:
        out_ctx = refs[pos]
        pos += 1
    st_refs = refs[pos:pos + 2 * hp]
    pos += 2 * hp
    acc_lat = refs[pos]
    acc_ctx = refs[pos + 1] if emit_ctx else None

    row = lax.broadcasted_iota(jnp.int32, (CHUNK, CHUNK), 0)
    col = lax.broadcasted_iota(jnp.int32, (CHUNK, CHUNK), 1)
    keep_f, keep_b = col <= row, col >= row
    tri_f = jnp.where(keep_f, 1.0, 0.0).astype(BF16)
    tri_b = jnp.where(keep_b, 1.0, 0.0).astype(BF16)
    dirs = (("f", tri_f, keep_f, CHUNK // 2 - 1, CHUNK - 1), ("b", tri_b, keep_b, CHUNK // 2, 0))

    if mode == "hgrn":
        def lower_bound(tab_ref):
            t = tab_ref[...]
            e = jnp.exp(t - jnp.max(t, axis=0, keepdims=True))
            s = e / jnp.sum(e, axis=0, keepdims=True)
            return (s[0:1] + s[1:2]) - s[0:1]
        lbs = {"f": lower_bound(lbf_ref), "b": lower_bound(lbb_ref)}

    def load(stream, rows, d, h):
        kc = slice(h * dk, (h + 1) * dk)
        vc = slice(h * dv, (h + 1) * dv)
        if mode == "gla":
            q_ref, k_ref, v_ref, _, r_ref = stream
            wg_ref, bg_ref = (wgf_ref, bgf_ref) if d == "f" else (wgb_ref, bgb_ref)
            q = q_ref[rows, kc].astype(F32) * (dk ** -0.5)
            k = k_ref[rows, kc].astype(F32)
            pre = jnp.dot(r_ref[rows, :].astype(BF16), wg_ref[h], preferred_element_type=F32) + bg_ref[h]
            g = _log_sigmoid(pre) * (1.0 / GLA_GATE_NORM)
        else:
            q_ref, v_ref, zf_ref, zb_ref, _ = stream
            z_ref = zf_ref if d == "f" else zb_ref
            q = _silu(q_ref[rows, kc].astype(F32)) * (dk ** -0.5)
            lb = lbs[d][:, kc]
            f = lb + (1.0 - lb) * jax.nn.sigmoid(z_ref[rows, kc])
            k = 1.0 - f
            g = jnp.log(f)
        return q, k, v_ref[rows, vc], g

    def gate_of(stream):
        return stream[3] if mode == "gla" else stream[4]

    def run(stream, n_chunks, emit, acc_ref, out_ref):
        half = n_chunks // 2
        per_step = min(SCAN_UNROLL, half)

        def make_body(final):
            def body(c, carry):
                jobs = []
                for u in range(per_step):
                    step = c * per_step + u
                    for di, (d, tri, keep, mid, last) in enumerate(dirs):
                        cc = step if d == "f" else n_chunks - 1 - step
                        rows = pl.ds(pl.multiple_of(cc * CHUNK, CHUNK), CHUNK)
                        for h in range(hp):
                            q, k, vb, g = load(stream, rows, d, h)
                            jobs.append(dict(q=q, k=k, vb=vb, g=g, tri=tri, keep=keep, mid=mid, last=last,
                                             chain=di * hp + h, rows=rows, vc=slice(h * dv, (h + 1) * dv)))
                outs = _chunk_steps(jobs, st_refs, emit)
                if emit:
                    for jb, o in zip(jobs, outs):
                        rows, vc = jb["rows"], jb["vc"]
                        if final:
                            gate = gate_of(stream)[rows, vc].astype(F32)
                            out_ref[rows, vc] = _readout(acc_ref[rows, vc] + o, gate,
                                                         norm_ref[...]).astype(out_ref.dtype)
                        else:
                            acc_ref[rows, vc] = o
                return carry
            return body

        lax.fori_loop(0, half // per_step, make_body(False), 0)
        lax.fori_loop(half // per_step, n_chunks // per_step, make_body(True), 0)

    for st_ref in st_refs:
        st_ref[...] = jnp.zeros(st_ref.shape, F32)
    run(ctx, n_ctx, emit_ctx, acc_ctx, out_ctx)
    run(lat, n_lat, True, acc_lat, out_lat)


def bidir_scan(mode, proj_lat, proj_ctx, extras, emit_ctx, hp=2):
    if mode == "gla":
        heads, dk, dv = GLA_HEADS, GLA_DK, GLA_DV
        nb = heads // hp

        def stream(proj, rows):
            main, r = proj
            specs = [pl.BlockSpec((rows, hp * dk), lambda b, h: (b, h)),
                     pl.BlockSpec((rows, hp * dk), lambda b, h: (b, nb + h)),
                     pl.BlockSpec((rows, hp * dv), lambda b, h: (b, nb + h)),
                     pl.BlockSpec((rows, hp * dv), lambda b, h: (b, 2 * nb + h)),
                     pl.BlockSpec((rows, LANES), lambda b, h: (b, 0))]
            return specs, [main, main, main, main, r]

        wgf, wgb, bgf, bgb, norm = extras
        sq = pl.BlockSpec((hp, LANES, dk), lambda b, h: (h, 0, 0))
        bias = pl.BlockSpec((hp, 1, dk), lambda b, h: (h, 0, 0))
        extra_specs = [sq, sq, bias, bias, pl.BlockSpec((1, dv), lambda b, h: (0, 0))]
        extra_args = [wgf, wgb, bgf, bgb, norm.reshape(1, dv)]
    else:
        heads, dk, dv = HG_HEADS, HG_DK, HG_DV
        nb = heads // hp

        def stream(proj, rows):
            pq, pz, pg = proj
            blk = lambda off: pl.BlockSpec((rows, hp * dk), lambda b, h: (b, off + h))
            return [blk(0), blk(nb), blk(0), blk(nb), blk(0)], [pq, pq, pz, pz, pg]

        lbf, lbb, norm = extras
        tab = pl.BlockSpec((DEPTH, hp * dk), lambda b, h: (0, h))
        extra_specs = [tab, tab, pl.BlockSpec((1, dv), lambda b, h: (0, 0))]
        extra_args = [lbf, lbb, norm.reshape(1, dv)]

    ls, la = stream(proj_lat, SEQ)
    cs, ca = stream(proj_ctx, CTX_LEN)
    out_shape = [jax.ShapeDtypeStruct((BATCH * SEQ, heads * dv), BF16)]
    out_specs = [pl.BlockSpec((SEQ, hp * dv), lambda b, h: (b, h))]
    scratch = [pltpu.VMEM((dv, dk), F32) for _ in range(2 * hp)] + [pltpu.VMEM((SEQ, hp * dv), F32)]
    if emit_ctx:
        out_shape.append(jax.ShapeDtypeStruct((BATCH * CTX_LEN, heads * dv), BF16))
        out_specs.append(pl.BlockSpec((CTX_LEN, hp * dv), lambda b, h: (b, h)))
        scratch.append(pltpu.VMEM((CTX_LEN, hp * dv), F32))
    outs = pl.pallas_call(
        functools.partial(_scan_kernel, mode=mode, hp=hp, dk=dk, dv=dv, n_lat=SEQ // CHUNK,
                          n_ctx=CTX_LEN // CHUNK, emit_ctx=emit_ctx),
        out_shape=out_shape,
        grid=(BATCH, nb),
        in_specs=ls + cs + extra_specs,
        out_specs=out_specs,
        scratch_shapes=scratch,
        compiler_params=_params(("arbitrary", "arbitrary")),
        name="bidir_scan_" + mode,
    )(*la, *ca, *extra_args)
    return outs if emit_ctx else (outs[0], None)


def _mm_ln_kernel(*refs, nk, with_h):
    if with_h:
        a_ref, w_ref, x_ref, gate_ref, g_ref, b_ref, sh_ref, sc_ref, xo_ref, ho_ref, hst_ref = refs
    else:
        a_ref, w_ref, x_ref, gate_ref, g_ref, b_ref, xo_ref = refs
    kk = pl.program_id(1)
    tn = 512

    def accumulate(first):
        a = a_ref[...]
        for n in range(D_MODEL // tn):
            cols = slice(n * tn, (n + 1) * tn)
            part = jnp.dot(a, w_ref[:, cols], preferred_element_type=F32)
            if first:
                xo_ref[:, cols] = part
            else:
                xo_ref[:, cols] += part

    pl.when(kk == 0)(functools.partial(accumulate, True))
    pl.when(kk > 0)(functools.partial(accumulate, False))

    @pl.when(kk == nk - 1)
    def _():
        def body(r, carry):
            base = r * (LN_GROUPS * SUBLANES)
            rows = [pl.ds(pl.multiple_of(base + t * SUBLANES, SUBLANES), SUBLANES) for t in range(LN_GROUPS)]
            ys = [ALPHA * x_ref[rw, :] + gate_ref[...] * xo_ref[rw, :] for rw in rows]
            mus = [jnp.mean(y, axis=-1, keepdims=True) for y in ys]
            dls = [y - mu for y, mu in zip(ys, mus)]
            rss = [lax.rsqrt(jnp.mean(d * d, axis=-1, keepdims=True) + LN_EPS) for d in dls]
            zs = [d * rs * g_ref[...] + b_ref[...] for d, rs in zip(dls, rss)]
            for rw, z in zip(rows, zs):
                xo_ref[rw, :] = z
            if with_h:
                for t, z in enumerate(zs):
                    hst_ref[t * SUBLANES:(t + 1) * SUBLANES, :] = z * (1.0 + sc_ref[...]) + sh_ref[...]
                ho_ref[pl.ds(pl.multiple_of(base, LN_GROUPS * SUBLANES), LN_GROUPS * SUBLANES), :] = (
                    hst_ref[...].astype(ho_ref.dtype))
            return carry

        lax.fori_loop(0, xo_ref.shape[0] // (LN_GROUPS * SUBLANES), body, 0)


def matmul_layernorm(a, w, x, gate, ln_g, ln_b, next_mod, group_fn, tm, tk=DOWN_TK):
    m, k = a.shape
    assert m % tm == 0 and k % tk == 0
    nk = k // tk
    gfn = lambda i: group_fn(i, tm)
    in_specs = [pl.BlockSpec((tm, tk), lambda i, kk: (i, kk)),
                pl.BlockSpec((tk, D_MODEL), lambda i, kk: (kk, 0)),
                pl.BlockSpec((tm, D_MODEL), lambda i, kk: (i, 0)),
                _mod_spec(gate[1], gfn), _vec_spec(), _vec_spec()]
    args = [a, w, x, gate[0], ln_g.reshape(1, D_MODEL), ln_b.reshape(1, D_MODEL)]
    out_shape = [jax.ShapeDtypeStruct((m, D_MODEL), F32)]
    out_specs = [pl.BlockSpec((tm, D_MODEL), lambda i, kk: (i, 0))]
    if next_mod is not None:
        in_specs += [_mod_spec(next_mod[1], gfn), _mod_spec(next_mod[2], gfn)]
        args += [next_mod[0], next_mod[0]]
        out_shape.append(jax.ShapeDtypeStruct((m, D_MODEL), BF16))
        out_specs.append(pl.BlockSpec((tm, D_MODEL), lambda i, kk: (i, 0)))
    outs = pl.pallas_call(
        functools.partial(_mm_ln_kernel, nk=nk, with_h=next_mod is not None),
        out_shape=out_shape,
        grid=(m // tm, nk),
        in_specs=in_specs,
        out_specs=out_specs,
        scratch_shapes=[pltpu.VMEM((LN_GROUPS * SUBLANES, D_MODEL), F32)] if next_mod is not None else [],
        compiler_params=_params(("arbitrary", "arbitrary")),
        name="matmul_layernorm",
    )(*args)
    return (outs[0], outs[1]) if next_mod is not None else (outs[0], None)


def _ffn_up_kernel(*refs, tiles_per_seq, n_tiles, per_step):
    a_ref, prev_ref, next_ref = refs[:3]
    tile_refs = [refs[3 + 6 * t:9 + 6 * t] for t in range(per_step)]
    o_ref, aext_ref = refs[3 + 6 * per_step:]
    i, j = pl.program_id(0), pl.program_id(1)
    tm = a_ref.shape[0]
    ext = tm + 2 * HALO

    @pl.when(j == 0)
    def _():
        first = (i % tiles_per_seq) == 0
        last = (i % tiles_per_seq) == tiles_per_seq - 1
        zero = jnp.zeros(prev_ref.shape, prev_ref.dtype)
        aext_ref[0:HALO, :] = jnp.where(first, zero, prev_ref[...])
        aext_ref[HALO:HALO + tm, :] = a_ref[...]
        aext_ref[HALO + tm:, :] = jnp.where(last, zero, next_ref[...])

    a = aext_ref[...]
    us = [(jnp.dot(a, wg[...], preferred_element_type=F32), jnp.dot(a, wv[...], preferred_element_type=F32))
          for wg, wv, *_ in tile_refs]

    def conv(u, cw_ref, cb_ref):
        up = pltpu.roll(u, 1, 0)[HALO:HALO + tm]
        un = pltpu.roll(u, ext - 1, 0)[HALO:HALO + tm]
        return cw_ref[0:1, :] * up + cw_ref[1:2, :] * u[HALO:HALO + tm] + cw_ref[2:3, :] * un + cb_ref[...]

    for t, ((ug, uv), (_, _, cwg, cwv, cbg, cbv)) in enumerate(zip(us, tile_refs)):
        tn = ug.shape[1]
        act = _silu(conv(ug, cwg, cbg)) * conv(uv, cwv, cbv)
        act = jnp.where(j * per_step + t < n_tiles, act, 0.0)
        o_ref[:, t * tn:(t + 1) * tn] = act.astype(o_ref.dtype)


def ffn_up(h, w_up, conv_w, conv_b, seq_len, tm, tn=FFN_TN, per_step=2):
    m, k = h.shape
    n_tiles = D_FF // tn
    assert seq_len % tm == 0 and D_FF % tn == 0 and D_FF_PAD % (per_step * tn) == 0 and tm % HALO == 0
    hb = tm // HALO
    last_hb = m // HALO - 1
    cb = conv_b.reshape(1, 2 * D_FF)
    in_specs = [pl.BlockSpec((tm, k), lambda i, j: (i, 0)),
                pl.BlockSpec((HALO, k), lambda i, j: (jnp.maximum(i * hb - 1, 0), 0)),
                pl.BlockSpec((HALO, k), lambda i, j: (jnp.minimum((i + 1) * hb, last_hb), 0))]
    args = [h, h, h]
    for t in range(per_step):
        tc = lambda j, t=t: jnp.minimum(j * per_step + t, n_tiles - 1)
        in_specs += [pl.BlockSpec((k, tn), lambda i, j, tc=tc: (0, tc(j))),
                     pl.BlockSpec((k, tn), lambda i, j, tc=tc: (0, n_tiles + tc(j))),
                     pl.BlockSpec((CONV_W, tn), lambda i, j, tc=tc: (0, tc(j))),
                     pl.BlockSpec((CONV_W, tn), lambda i, j, tc=tc: (0, n_tiles + tc(j))),
                     pl.BlockSpec((1, tn), lambda i, j, tc=tc: (0, tc(j))),
                     pl.BlockSpec((1, tn), lambda i, j, tc=tc: (0, n_tiles + tc(j)))]
        args += [w_up, w_up, conv_w, conv_w, cb, cb]
    return pl.pallas_call(
        functools.partial(_ffn_up_kernel, tiles_per_seq=seq_len // tm, n_tiles=n_tiles, per_step=per_step),
        out_shape=jax.ShapeDtypeStruct((m, D_FF_PAD), BF16),
        grid=(m // tm, D_FF_PAD // (per_step * tn)),
        in_specs=in_specs,
        out_specs=pl.BlockSpec((tm, per_step * tn), lambda i, j: (i, j)),
        scratch_shapes=[pltpu.VMEM((tm + 2 * HALO, k), BF16)],
        compiler_params=_params(("arbitrary", "arbitrary")),
        name="ffn_up",
    )(*args)


def _rope_tables():
    rows = SEQ // GRID_W
    row = jnp.repeat(jnp.arange(rows, dtype=F32), GRID_W)
    col = jnp.tile(jnp.arange(GRID_W, dtype=F32), rows)
    half = ROPE_AXIS_DIM // 2
    inv = ROPE_THETA ** (-jnp.arange(half, dtype=F32) / half)
    ang_r, ang_c = row[:, None] * inv, col[:, None] * inv
    ang = jnp.concatenate([ang_r, ang_r, ang_c, ang_c], axis=-1)
    sign = jnp.tile(jnp.concatenate([-jnp.ones((half,), F32), jnp.ones((half,), F32)]), 2)
    return jnp.cos(ang), jnp.sin(ang) * sign


def _lat_group(i, tm):
    return i // (SEQ // tm)


def _ctx_group(i, tm):
    return CTX_GROUP


def _ffn(h_l, h_c, w_up, conv_w, conv_b):
    f_l = ffn_up(h_l, w_up, conv_w, conv_b, SEQ, tm=1024)
    f_c = ffn_up(h_c, w_up, conv_w, conv_b, CTX_LEN, tm=CTX_LEN) if h_c is not None else None
    return f_l, f_c


def kernel(x, c, ctx, c_ctx, hgrn_lb_f, hgrn_lb_b, l0_w_mod, l0_b_mod, l0_w_in, l0_gla_wg_f, l0_gla_bg_f, l0_gla_wg_b, l0_gla_bg_b, l0_da_lq1, l0_da_lk1, l0_da_lq2, l0_da_lk2, l0_da_norm, l0_gla_norm, l0_w_out, l0_ln1_g, l0_ln1_b, l0_ffn_up, l0_ffn_conv, l0_ffn_conv_b, l0_ffn_down, l0_ln2_g, l0_ln2_b, l1_w_mod, l1_b_mod, l1_w_in, l1_hg_norm, l1_w_out, l1_ln1_g, l1_ln1_b, l1_ffn_up, l1_ffn_conv, l1_ffn_conv_b, l1_ffn_down, l1_ln2_g, l1_ln2_b):
    xl = x.reshape(BATCH * SEQ, D_MODEL)
    xc = ctx.reshape(BATCH * CTX_LEN, D_MODEL)
    cond = jnp.concatenate([c, c_ctx[None, :], jnp.zeros((MOD_ROWS - BATCH - 1, D_MODEL), F32)], axis=0)
    mods0 = modulation_table(cond, l0_w_mod, l0_b_mod)
    mods1 = modulation_table(cond, l1_w_mod, l1_b_mod)
    bf = lambda w: w.astype(BF16)
    down = lambda w: jnp.concatenate([bf(w), jnp.zeros((D_FF_PAD - D_FF, D_MODEL), BF16)], axis=0)
    tm_l, tm_c = 512, 256
    pm_l, pm_c = 512, 256

    w_in = l0_w_in
    h_l = modulate(xl, mods0, _lat_group)
    h_c = modulate(xc, mods0, _ctx_group)
    q_cols = DA_HEADS * 2 * DA_DK
    pa_l = project(h_l, w_in, 0, ATTN_COLS, BF16, pm_l, q_cols=q_cols, rope=_rope_tables(), rope_cols=2 * q_cols)
    pa_c = project(h_c, w_in, 0, ATTN_COLS, BF16, pm_c, q_cols=q_cols)
    pb_l = project(h_l, w_in, ATTN_COLS, GLA_COLS, BF16, pm_l)
    pb_c = project(h_c, w_in, ATTN_COLS, GLA_COLS, BF16, pm_c)
    w_r = jnp.pad(l0_w_in[:, ATTN_COLS + GLA_COLS:], ((0, 0), (0, LANES - 2 * GLA_GATE_RANK)))
    pr_l = project(h_l, w_r, 0, LANES, F32, pm_l, tn=LANES)
    pr_c = project(h_c, w_r, 0, LANES, F32, pm_c, tn=LANES)

    lam_vecs = (l0_da_lq1, l0_da_lk1, l0_da_lq2, l0_da_lk2)
    oa_l = diff_attention(pa_l, [(pa_l, SEQ), (pa_c, CTX_LEN)], lam_vecs, l0_da_norm, SEQ, tq=256)
    oa_c = diff_attention(pa_c, [(pa_c, CTX_LEN)], lam_vecs, l0_da_norm, CTX_LEN, tq=CTX_LEN)

    def gate_weight(wg, row0):
        w = bf(wg).reshape(GLA_GATE_RANK, GLA_HEADS, GLA_DK).transpose(1, 0, 2)
        return jnp.pad(w, ((0, 0), (row0, LANES - GLA_GATE_RANK - row0), (0, 0)))

    gla_extras = (gate_weight(l0_gla_wg_f, 0), gate_weight(l0_gla_wg_b, GLA_GATE_RANK),
                  l0_gla_bg_f.reshape(GLA_HEADS, 1, GLA_DK), l0_gla_bg_b.reshape(GLA_HEADS, 1, GLA_DK), l0_gla_norm)
    ob_l, ob_c = bidir_scan("gla", (pb_l, pr_l), (pb_c, pr_c), gla_extras, emit_ctx=True)

    w_out = bf(l0_w_out)
    x1_l, hf_l = matmul_layernorm(jnp.concatenate([oa_l, ob_l], axis=1), w_out, xl, (mods0, 2), l0_ln1_g, l0_ln1_b,
                                  (mods0, 3, 4), _lat_group, tm=tm_l)
    x1_c, hf_c = matmul_layernorm(jnp.concatenate([oa_c, ob_c], axis=1), w_out, xc, (mods0, 2), l0_ln1_g, l0_ln1_b,
                                  (mods0, 3, 4), _ctx_group, tm=tm_c)
    f_l, f_c = _ffn(hf_l, hf_c, bf(l0_ffn_up), l0_ffn_conv, l0_ffn_conv_b)
    w_down = down(l0_ffn_down)
    x2_l, h1_l = matmul_layernorm(f_l, w_down, x1_l, (mods0, 5), l0_ln2_g, l0_ln2_b, (mods1, 0, 1), _lat_group, tm=tm_l)
    _, h1_c = matmul_layernorm(f_c, w_down, x1_c, (mods0, 5), l0_ln2_g, l0_ln2_b, (mods1, 0, 1), _ctx_group, tm=tm_c)

    w_in1 = l1_w_in

    def project1(h, tm):
        return (project(h, w_in1, 0, 2 * D_MODEL, BF16, tm), project(h, w_in1, 2 * D_MODEL, 2 * D_MODEL, F32, tm),
                project(h, w_in1, 4 * D_MODEL, D_MODEL, BF16, tm))

    o1_l, _ = bidir_scan("hgrn", project1(h1_l, pm_l), project1(h1_c, pm_c), (hgrn_lb_f, hgrn_lb_b, l1_hg_norm),
                         emit_ctx=False)
    x3_l, hf1_l = matmul_layernorm(o1_l, bf(l1_w_out), x2_l, (mods1, 2), l1_ln1_g, l1_ln1_b, (mods1, 3, 4),
                                   _lat_group, tm=tm_l)
    f1_l, _ = _ffn(hf1_l, None, bf(l1_ffn_up), l1_ffn_conv, l1_ffn_conv_b)
    x4_l, _ = matmul_layernorm(f1_l, down(l1_ffn_down), x3_l, (mods1, 5), l1_ln2_g, l1_ln2_b, None, _lat_group, tm=tm_l)
    return x4_l.reshape(BATCH, SEQ, D_MODEL)
```

```python
import functools
import math

import jax
import jax.numpy as jnp
from jax import lax
from jax.experimental import pallas as pl
from jax.experimental.pallas import tpu as pltpu

F32, BF16 = jnp.float32, jnp.bfloat16

D_MODEL = 4096
BATCH = 2
SEQ = 4096
DEPTH = 2
CTX_LEN = 256
GRID_W = 64
DA_HEADS = 8
DA_DK = 128
DA_DV = 2 * DA_DK
GLA_HEADS = 8
GLA_DK = 128
GLA_DV = 256
GLA_GATE_RANK = 16
GLA_GATE_NORM = 16.0
HG_DK = 128
HG_HEADS = D_MODEL // HG_DK
HG_DV = D_MODEL // HG_HEADS
CHUNK = 64
ROPE_THETA = 10000.0
ROPE_AXIS_DIM = DA_DK // 2
D_FF = ((8 * D_MODEL // 3 + 255) // 256) * 256
CONV_W = 3
ALPHA = (2 * DEPTH) ** 0.25
LN_EPS = 1e-5
RMS_EPS = 1e-5
LAM_INIT = 0.8 - 0.6 * math.exp(-0.3 * 0)

ATTN_COLS = DA_HEADS * (2 * DA_DK + 2 * DA_DK + DA_DV)
GLA_COLS = GLA_HEADS * (2 * GLA_DK + 2 * GLA_DV)
LANES = 128
SUBLANES = 8
BF16_ROWS = 16
HALO = BF16_ROWS
MOD_ROWS = 8
CTX_GROUP = BATCH
VMEM_LIMIT = 56 * 1024 * 1024
PROJ_VMEM_LIMIT = 58 * 1024 * 1024
FFN_TN = 256
PROJ_TN = 1024
LN_GROUPS = 4
ATTN_CHUNK = 512
DOWN_TK = 512
SCAN_UNROLL = 4
D_FF_PAD = -(-D_FF // DOWN_TK) * DOWN_TK
Q_PRESCALE = DA_DK ** -0.5 * math.log2(math.e)


def _params(sem, vmem=VMEM_LIMIT):
    return pltpu.CompilerParams(dimension_semantics=sem, vmem_limit_bytes=vmem)


def _silu(x):
    return x * jax.nn.sigmoid(x)


def _log_sigmoid(x):
    return -(jnp.maximum(-x, 0.0) + jnp.log1p(jnp.exp(-jnp.abs(x))))


def _mod_spec(which, group_fn):
    return pl.BlockSpec((None, None, 1, D_MODEL), lambda i, *_: (group_fn(i), which, 0, 0))


def _vec_spec():
    return pl.BlockSpec((1, D_MODEL), lambda i, *_: (0, 0))


def _mod_kernel(c_ref, w_ref, b_ref, o_ref):
    s = _silu(c_ref[...]).astype(BF16)
    o_ref[...] = jnp.dot(s, w_ref[...].astype(BF16), preferred_element_type=F32) + b_ref[...]


def modulation_table(cond, w_mod, b_mod, tn=512):
    n = w_mod.shape[1]
    out = pl.pallas_call(
        _mod_kernel,
        out_shape=jax.ShapeDtypeStruct((MOD_ROWS, n), F32),
        grid=(n // tn,),
        in_specs=[pl.BlockSpec((MOD_ROWS, D_MODEL), lambda j: (0, 0)),
                  pl.BlockSpec((D_MODEL, tn), lambda j: (0, j)),
                  pl.BlockSpec((1, tn), lambda j: (0, j))],
        out_specs=pl.BlockSpec((MOD_ROWS, tn), lambda j: (0, j)),
        compiler_params=_params(("arbitrary",)),
        name="modulation",
    )(cond, w_mod, b_mod.reshape(1, n))
    return out.reshape(MOD_ROWS, 6, 1, D_MODEL)


def _modulate_kernel(x_ref, sh_ref, sc_ref, o_ref):
    o_ref[...] = (x_ref[...] * (1.0 + sc_ref[...]) + sh_ref[...]).astype(o_ref.dtype)


def modulate(x, mods, group_fn, tm=256):
    m = x.shape[0]
    return pl.pallas_call(
        _modulate_kernel,
        out_shape=jax.ShapeDtypeStruct((m, D_MODEL), BF16),
        grid=(m // tm,),
        in_specs=[pl.BlockSpec((tm, D_MODEL), lambda i: (i, 0)),
                  _mod_spec(0, lambda i: group_fn(i, tm)), _mod_spec(1, lambda i: group_fn(i, tm))],
        out_specs=pl.BlockSpec((tm, D_MODEL), lambda i: (i, 0)),
        compiler_params=_params(("arbitrary",)),
        name="modulate",
    )(x, mods, mods)


def _proj_kernel(*refs, n_q_tiles, n_rope_tiles, w_transposed):
    if n_rope_tiles:
        a_ref, w_ref, cos_ref, sin_ref, o_ref, wbf_ref = refs
    else:
        a_ref, w_ref, o_ref, wbf_ref = refs
    j, i = pl.program_id(0), pl.program_id(1)

    @pl.when(i == 0)
    def _():
        wbf_ref[...] = w_ref[...].astype(BF16)

    contract = (((1,), (1 if w_transposed else 0,)), ((), ()))
    acc = lax.dot_general(a_ref[...], wbf_ref[...], contract, preferred_element_type=F32)
    tm, tn = acc.shape
    if n_q_tiles:
        acc = acc * jnp.where(j < n_q_tiles, Q_PRESCALE, 1.0)
    if not n_rope_tiles:
        o_ref[...] = acc.astype(o_ref.dtype)
        return

    @pl.when(j < n_rope_tiles)
    def _():
        cos, sin = cos_ref[...], sin_ref[...]
        lane = lax.broadcasted_iota(jnp.int32, (tm, LANES), 1)
        first_half = (lane % (2 * (ROPE_AXIS_DIM // 2))) < (ROPE_AXIS_DIM // 2)
        for c in range(tn // LANES):
            xs = acc[:, c * LANES:(c + 1) * LANES]
            partner = jnp.where(first_half, pltpu.roll(xs, LANES - ROPE_AXIS_DIM // 2, 1),
                                pltpu.roll(xs, ROPE_AXIS_DIM // 2, 1))
            o_ref[:, c * LANES:(c + 1) * LANES] = (xs * cos + partner * sin).astype(o_ref.dtype)

    @pl.when(j >= n_rope_tiles)
    def _():
        o_ref[...] = acc.astype(o_ref.dtype)


def project(a, w, col0, n, out_dtype, tm, tn=PROJ_TN, q_cols=0, rope=None, rope_cols=0, w_transposed=False):
    m, k = a.shape
    assert m % tm == 0 and n % tn == 0 and col0 % tn == 0 and q_cols % tn == 0
    w_block = (tn, k) if w_transposed else (k, tn)
    w_index = (lambda j, i: (col0 // tn + j, 0)) if w_transposed else (lambda j, i: (0, col0 // tn + j))
    in_specs = [pl.BlockSpec((tm, k), lambda j, i: (i, 0)), pl.BlockSpec(w_block, w_index)]
    args = [a, w]
    if rope is not None:
        assert rope_cols % tn == 0 and SEQ % tm == 0
        tab = pl.BlockSpec((tm, LANES), lambda j, i: (i % (SEQ // tm), 0))
        in_specs += [tab, tab]
        args += list(rope)
    return pl.pallas_call(
        functools.partial(_proj_kernel, n_q_tiles=q_cols // tn, n_rope_tiles=rope_cols // tn,
                          w_transposed=w_transposed),
        out_shape=jax.ShapeDtypeStruct((m, n), out_dtype),
        grid=(n // tn, m // tm),
        in_specs=in_specs,
        out_specs=pl.BlockSpec((tm, tn), lambda j, i: (i, j)),
        scratch_shapes=[pltpu.VMEM(w_block, BF16)],
        compiler_params=_params(("arbitrary", "arbitrary"), vmem=PROJ_VMEM_LIMIT),
        name="project",
    )(*args)


def _convert_pad_kernel(w_ref, o_ref, *, n_blocks):
    o_ref[...] = jnp.where(pl.program_id(0) < n_blocks, w_ref[...], 0.0).astype(o_ref.dtype)


def convert_pad_rows(w, rows_out, tr=256):
    r, c = w.shape
    assert r % tr == 0 and rows_out % tr == 0
    n_blocks = r // tr
    return pl.pallas_call(
        functools.partial(_convert_pad_kernel, n_blocks=n_blocks),
        out_shape=jax.ShapeDtypeStruct((rows_out, c), BF16),
        grid=(rows_out // tr,),
        in_specs=[pl.BlockSpec((tr, c), lambda i: (jnp.minimum(i, n_blocks - 1), 0))],
        out_specs=pl.BlockSpec((tr, c), lambda i: (i, 0)),
        compiler_params=_params(("arbitrary",)),
        name="convert_pad_rows",
    )(w)


def _attn_kernel(*refs, n_src):
    lq1, lk1, lq2, lk2, norm_ref, q_ref = refs[:6]
    k_refs = refs[6:6 + n_src]
    v_refs = refs[6 + n_src:6 + 2 * n_src]
    o_ref = refs[6 + 2 * n_src]
    s_ref = refs[7 + 2 * n_src]
    lam = (jnp.exp(jnp.sum(lq1[...] * lk1[...], axis=-1, keepdims=True))
           - jnp.exp(jnp.sum(lq2[...] * lk2[...], axis=-1, keepdims=True)) + LAM_INIT)
    chunks, col = [], 0
    for kr, vr in zip(k_refs, v_refs):
        for r0 in range(0, kr.shape[0], ATTN_CHUNK):
            n = min(ATTN_CHUNK, kr.shape[0] - r0)
            chunks.append((kr, vr, r0, n, col))
            col += n
    nt = (((1,), (1,)), ((), ()))

    def fold(x, op):
        acc = x[:, :LANES]
        for c in range(LANES, x.shape[1], LANES):
            acc = op(acc, x[:, c:c + LANES])
        return acc

    def scores(mp, ch):
        kr, _, r0, n, c0 = ch
        s = lax.dot_general(q_ref[:, mp * DA_DK:(mp + 1) * DA_DK],
                            kr[r0:r0 + n, mp * DA_DK:(mp + 1) * DA_DK], nt, preferred_element_type=F32)
        s_ref[mp, :, c0:c0 + n] = s
        return fold(s, jnp.maximum)

    def weighted_values(mp, ch, mx):
        _, vr, r0, n, c0 = ch
        p = jnp.exp2(s_ref[mp, :, c0:c0 + n] - mx)
        return fold(p, jnp.add), jnp.dot(p.astype(BF16), vr[r0:r0 + n, :], preferred_element_type=F32)

    def merge(acc, new, op):
        return new if acc is None else op(acc, new)

    m1 = m2 = l1 = l2 = o1 = o2 = None
    for ch in chunks:
        m1 = merge(m1, scores(0, ch), jnp.maximum)
    mx1 = m1.max(axis=-1, keepdims=True)
    for ch in chunks:
        m2 = merge(m2, scores(1, ch), jnp.maximum)
        p, pv = weighted_values(0, ch, mx1)
        l1, o1 = merge(l1, p, jnp.add), merge(o1, pv, jnp.add)
    mx2 = m2.max(axis=-1, keepdims=True)
    for ch in chunks:
        p, pv = weighted_values(1, ch, mx2)
        l2, o2 = merge(l2, p, jnp.add), merge(o2, pv, jnp.add)
    o = o1 * (1.0 / l1.sum(axis=-1, keepdims=True)) - o2 * (lam / l2.sum(axis=-1, keepdims=True))
    ms = jnp.mean(o * o, axis=-1, keepdims=True)
    o_ref[...] = (o * lax.rsqrt(ms + RMS_EPS) * norm_ref[...] * (1.0 - LAM_INIT)).astype(o_ref.dtype)


def diff_attention(q_src, kv_srcs, lam_vecs, da_norm, t_q, tq):
    nq = t_q // tq
    hq = 2 * DA_DK
    assert all(rows % LANES == 0 for _, rows in kv_srcs)
    vec = pl.BlockSpec((1, DA_DK), lambda b, h, i: (0, 0))
    in_specs = [vec, vec, vec, vec, pl.BlockSpec((1, DA_DV), lambda b, h, i: (0, 0)),
                pl.BlockSpec((tq, hq), lambda b, h, i: (b * nq + i, h))]
    args = [v.reshape(1, DA_DK) for v in lam_vecs] + [da_norm.reshape(1, DA_DV), q_src]
    for off in (DA_HEADS, 2 * DA_HEADS):
        for arr, rows in kv_srcs:
            in_specs.append(pl.BlockSpec((rows, hq), lambda b, h, i, off=off: (b, off + h)))
            args.append(arr)
    return pl.pallas_call(
        functools.partial(_attn_kernel, n_src=len(kv_srcs)),
        out_shape=jax.ShapeDtypeStruct((BATCH * t_q, DA_HEADS * DA_DV), BF16),
        grid=(BATCH, DA_HEADS, nq),
        in_specs=in_specs,
        out_specs=pl.BlockSpec((tq, DA_DV), lambda b, h, i: (b * nq + i, h)),
        scratch_shapes=[pltpu.VMEM((2, tq, sum(rows for _, rows in kv_srcs)), F32)],
        compiler_params=_params(("arbitrary", "arbitrary", "arbitrary")),
        name="diff_attention",
    )(*args)


def _chunk_steps(jobs, st_refs, emit):
    nt = (((1,), (1,)), ((), ()))
    tn = (((0,), (0,)), ((), ()))
    for jb in jobs:
        g = jb["g"]
        g_hi = g.astype(BF16)
        r1 = g - g_hi.astype(F32)
        g_mid = r1.astype(BF16)
        g_lo = (r1 - g_mid.astype(F32)).astype(BF16)
        jb["cs"] = jnp.dot(jb["tri"], jnp.concatenate([g_hi, g_mid, g_lo], axis=1), preferred_element_type=F32)
    for jb in jobs:
        q, k, cs = jb["q"], jb["k"], jb["cs"]
        dk = q.shape[-1]
        bc = cs[:, :dk] + cs[:, dk:2 * dk] + cs[:, 2 * dk:]
        b_last = bc[jb["last"]:jb["last"] + 1, :]
        jb["decay"] = jnp.exp(b_last)
        k_carry = (k * jnp.exp(b_last - bc)).astype(BF16)
        jb["upd"] = lax.dot_general(jb["vb"], k_carry, tn, preferred_element_type=F32)
        if emit:
            b_mid = bc[jb["mid"]:jb["mid"] + 1, :]
            qs = (q * jnp.exp(bc - b_mid)).astype(BF16)
            ks = (k * jnp.exp(b_mid - bc)).astype(BF16)
            jb["sc"] = lax.dot_general(qs, ks, nt, preferred_element_type=F32)
            jb["qi"] = (q * jnp.exp(bc)).astype(BF16)
    st = {}
    for jb in jobs:
        ch = jb["chain"]
        if ch not in st:
            st[ch] = st_refs[ch][...]
        jb["st_in"] = st[ch]
        st[ch] = jb["decay"] * st[ch] + jb["upd"]
    for ch, val in st.items():
        st_refs[ch][...] = val
    if not emit:
        return None
    outs = []
    for jb in jobs:
        sc = jnp.where(jb["keep"], jb["sc"], 0.0).astype(BF16)
        outs.append(jnp.dot(sc, jb["vb"], preferred_element_type=F32)
                    + lax.dot_general(jb["qi"], jb["st_in"].astype(BF16), nt, preferred_element_type=F32))
    return outs


def _readout(o, gate, norm):
    ms = jnp.mean(o * o, axis=-1, keepdims=True)
    return o * lax.rsqrt(ms + RMS_EPS) * norm * _silu(gate)


def _scan_kernel(*refs, mode, hp, dk, dv, n_lat, n_ctx, emit_ctx):
    n_stream = 5
    lat, ctx = refs[:n_stream], refs[n_stream:2 * n_stream]
    pos = 2 * n_stream
    if mode == "gla":
        wgf_ref, wgb_ref, bgf_ref, bgb_ref, norm_ref = refs[pos:pos + 5]
        pos += 5
    else:
        lbf_ref, lbb_ref, norm_ref = refs[pos:pos + 3]
        pos += 3
    out_lat = refs[pos]
    pos += 1
    out_ctx = None
    if emit_ctx:
        out_ctx = refs[pos]
        pos += 1
    st_refs = refs[pos:pos + 2 * hp]
    pos += 2 * hp
    acc_lat = refs[pos]
    acc_ctx = refs[pos + 1] if emit_ctx else None

    row = lax.broadcasted_iota(jnp.int32, (CHUNK, CHUNK), 0)
    col = lax.broadcasted_iota(jnp.int32, (CHUNK, CHUNK), 1)
    keep_f, keep_b = col <= row, col >= row
    tri_f = jnp.where(keep_f, 1.0, 0.0).astype(BF16)
    tri_b = jnp.where(keep_b, 1.0, 0.0).astype(BF16)
    dirs = (("f", tri_f, keep_f, CHUNK // 2 - 1, CHUNK - 1), ("b", tri_b, keep_b, CHUNK // 2, 0))

    if mode == "hgrn":
        def lower_bound(tab_ref):
            t = tab_ref[...]
            e = jnp.exp(t - jnp.max(t, axis=0, keepdims=True))
            s = e / jnp.sum(e, axis=0, keepdims=True)
            return (s[0:1] + s[1:2]) - s[0:1]
        lbs = {"f": lower_bound(lbf_ref), "b": lower_bound(lbb_ref)}

    def load(stream, rows, d, h):
        kc = slice(h * dk, (h + 1) * dk)
        vc = slice(h * dv, (h + 1) * dv)
        if mode == "gla":
            q_ref, k_ref, v_ref, _, r_ref = stream
            wg_ref, bg_ref = (wgf_ref, bgf_ref) if d == "f" else (wgb_ref, bgb_ref)
            q = q_ref[rows, kc].astype(F32) * (dk ** -0.5)
            k = k_ref[rows, kc].astype(F32)
            pre = jnp.dot(r_ref[rows, :].astype(BF16), wg_ref[h], preferred_element_type=F32) + bg_ref[h]
            g = _log_sigmoid(pre) * (1.0 / GLA_GATE_NORM)
        else:
            q_ref, v_ref, zf_ref, zb_ref, _ = stream
            z_ref = zf_ref if d == "f" else zb_ref
            q = _silu(q_ref[rows, kc].astype(F32)) * (dk ** -0.5)
            lb = lbs[d][:, kc]
            f = lb + (1.0 - lb) * jax.nn.sigmoid(z_ref[rows, kc])
            k = 1.0 - f
            g = jnp.log(f)
        return q, k, v_ref[rows, vc], g

    def gate_of(stream):
        return stream[3] if mode == "gla" else stream[4]

    def run(stream, n_chunks, emit, acc_ref, out_ref):
        half = n_chunks // 2
        per_step = min(SCAN_UNROLL, half)

        def make_body(final):
            def body(c, carry):
                jobs = []
                for u in range(per_step):
                    step = c * per_step + u
                    for di, (d, tri, keep, mid, last) in enumerate(dirs):
                        cc = step if d == "f" else n_chunks - 1 - step
                        rows = pl.ds(pl.multiple_of(cc * CHUNK, CHUNK), CHUNK)
                        for h in range(hp):
                            q, k, vb, g = load(stream, rows, d, h)
                            jobs.append(dict(q=q, k=k, vb=vb, g=g, tri=tri, keep=keep, mid=mid, last=last,
                                             chain=di * hp + h, rows=rows, vc=slice(h * dv, (h + 1) * dv)))
                outs = _chunk_steps(jobs, st_refs, emit)
                if emit:
                    for jb, o in zip(jobs, outs):
                        rows, vc = jb["rows"], jb["vc"]
                        if final:
                            gate = gate_of(stream)[rows, vc].astype(F32)
                            out_ref[rows, vc] = _readout(acc_ref[rows, vc] + o, gate,
                                                         norm_ref[...]).astype(out_ref.dtype)
                        else:
                            acc_ref[rows, vc] = o
                return carry
            return body

        lax.fori_loop(0, half // per_step, make_body(False), 0)
        lax.fori_loop(half // per_step, n_chunks // per_step, make_body(True), 0)

    for st_ref in st_refs:
        st_ref[...] = jnp.zeros(st_ref.shape, F32)
    run(ctx, n_ctx, emit_ctx, acc_ctx, out_ctx)
    run(lat, n_lat, True, acc_lat, out_lat)


def bidir_scan(mode, proj_lat, proj_ctx, extras, emit_ctx, hp=2):
    if mode == "gla":
        heads, dk, dv = GLA_HEADS, GLA_DK, GLA_DV
        nb = heads // hp

        def stream(proj, rows):
            main, r = proj
            specs = [pl.BlockSpec((rows, hp * dk), lambda b, h: (b, h)),
                     pl.BlockSpec((rows, hp * dk), lambda b, h: (b, nb + h)),
                     pl.BlockSpec((rows, hp * dv), lambda b, h: (b, nb + h)),
                     pl.BlockSpec((rows, hp * dv), lambda b, h: (b, 2 * nb + h)),
                     pl.BlockSpec((rows, LANES), lambda b, h: (b, 0))]
            return specs, [main, main, main, main, r]

        wgf, wgb, bgf, bgb, norm = extras
        sq = pl.BlockSpec((hp, LANES, dk), lambda b, h: (h, 0, 0))
        bias = pl.BlockSpec((hp, 1, dk), lambda b, h: (h, 0, 0))
        extra_specs = [sq, sq, bias, bias, pl.BlockSpec((1, dv), lambda b, h: (0, 0))]
        extra_args = [wgf, wgb, bgf, bgb, norm.reshape(1, dv)]
    else:
        heads, dk, dv = HG_HEADS, HG_DK, HG_DV
        nb = heads // hp

        def stream(proj, rows):
            pq, pz, pg = proj
            blk = lambda off: pl.BlockSpec((rows, hp * dk), lambda b, h: (b, off + h))
            return [blk(0), blk(nb), blk(0), blk(nb), blk(0)], [pq, pq, pz, pz, pg]

        lbf, lbb, norm = extras
        tab = pl.BlockSpec((DEPTH, hp * dk), lambda b, h: (0, h))
        extra_specs = [tab, tab, pl.BlockSpec((1, dv), lambda b, h: (0, 0))]
        extra_args = [lbf, lbb, norm.reshape(1, dv)]

    ls, la = stream(proj_lat, SEQ)
    cs, ca = stream(proj_ctx, CTX_LEN)
    out_shape = [jax.ShapeDtypeStruct((BATCH * SEQ, heads * dv), BF16)]
    out_specs = [pl.BlockSpec((SEQ, hp * dv), lambda b, h: (b, h))]
    scratch = [pltpu.VMEM((dv, dk), F32) for _ in range(2 * hp)] + [pltpu.VMEM((SEQ, hp * dv), F32)]
    if emit_ctx:
        out_shape.append(jax.ShapeDtypeStruct((BATCH * CTX_LEN, heads * dv), BF16))
        out_specs.append(pl.BlockSpec((CTX_LEN, hp * dv), lambda b, h: (b, h)))
        scratch.append(pltpu.VMEM((CTX_LEN, hp * dv), F32))
    outs = pl.pallas_call(
        functools.partial(_scan_kernel, mode=mode, hp=hp, dk=dk, dv=dv, n_lat=SEQ // CHUNK,
                          n_ctx=CTX_LEN // CHUNK, emit_ctx=emit_ctx),
        out_shape=out_shape,
        grid=(BATCH, nb),
        in_specs=ls + cs + extra_specs,
        out_specs=out_specs,
        scratch_shapes=scratch,
        compiler_params=_params(("arbitrary", "arbitrary")),
        name="bidir_scan_" + mode,
    )(*la, *ca, *extra_args)
    return outs if emit_ctx else (outs[0], None)


def _mm_ln_kernel(*refs, nk, with_h):
    if with_h:
        a_ref, w_ref, x_ref, gate_ref, g_ref, b_ref, sh_ref, sc_ref, xo_ref, ho_ref, hst_ref = refs
    else:
        a_ref, w_ref, x_ref, gate_ref, g_ref, b_ref, xo_ref = refs
    kk = pl.program_id(1)
    tn = 512

    def accumulate(first):
        a = a_ref[...]
        for n in range(D_MODEL // tn):
            cols = slice(n * tn, (n + 1) * tn)
            part = jnp.dot(a, w_ref[:, cols], preferred_element_type=F32)
            if first:
                xo_ref[:, cols] = part
            else:
                xo_ref[:, cols] += part

    pl.when(kk == 0)(functools.partial(accumulate, True))
    pl.when(kk > 0)(functools.partial(accumulate, False))

    @pl.when(kk == nk - 1)
    def _():
        def body(r, carry):
            base = r * (LN_GROUPS * SUBLANES)
            rows = [pl.ds(pl.multiple_of(base + t * SUBLANES, SUBLANES), SUBLANES) for t in range(LN_GROUPS)]
            ys = [ALPHA * x_ref[rw, :] + gate_ref[...] * xo_ref[rw, :] for rw in rows]
            mus = [jnp.mean(y, axis=-1, keepdims=True) for y in ys]
            dls = [y - mu for y, mu in zip(ys, mus)]
            rss = [lax.rsqrt(jnp.mean(d * d, axis=-1, keepdims=True) + LN_EPS) for d in dls]
            zs = [d * rs * g_ref[...] + b_ref[...] for d, rs in zip(dls, rss)]
            for rw, z in zip(rows, zs):
                xo_ref[rw, :] = z
            if with_h:
                for t, z in enumerate(zs):
                    hst_ref[t * SUBLANES:(t + 1) * SUBLANES, :] = z * (1.0 + sc_ref[...]) + sh_ref[...]
                ho_ref[pl.ds(pl.multiple_of(base, LN_GROUPS * SUBLANES), LN_GROUPS * SUBLANES), :] = (
                    hst_ref[...].astype(ho_ref.dtype))
            return carry

        lax.fori_loop(0, xo_ref.shape[0] // (LN_GROUPS * SUBLANES), body, 0)


def matmul_layernorm(a, w, x, gate, ln_g, ln_b, next_mod, group_fn, tm, tk=DOWN_TK):
    m, k = a.shape
    assert m % tm == 0 and k % tk == 0
    nk = k // tk
    gfn = lambda i: group_fn(i, tm)
    in_specs = [pl.BlockSpec((tm, tk), lambda i, kk: (i, kk)),
                pl.BlockSpec((tk, D_MODEL), lambda i, kk: (kk, 0)),
                pl.BlockSpec((tm, D_MODEL), lambda i, kk: (i, 0)),
                _mod_spec(gate[1], gfn), _vec_spec(), _vec_spec()]
    args = [a, w, x, gate[0], ln_g.reshape(1, D_MODEL), ln_b.reshape(1, D_MODEL)]
    out_shape = [jax.ShapeDtypeStruct((m, D_MODEL), F32)]
    out_specs = [pl.BlockSpec((tm, D_MODEL), lambda i, kk: (i, 0))]
    if next_mod is not None:
        in_specs += [_mod_spec(next_mod[1], gfn), _mod_spec(next_mod[2], gfn)]
        args += [next_mod[0], next_mod[0]]
        out_shape.append(jax.ShapeDtypeStruct((m, D_MODEL), BF16))
        out_specs.append(pl.BlockSpec((tm, D_MODEL), lambda i, kk: (i, 0)))
    outs = pl.pallas_call(
        functools.partial(_mm_ln_kernel, nk=nk, with_h=next_mod is not None),
        out_shape=out_shape,
        grid=(m // tm, nk),
        in_specs=in_specs,
        out_specs=out_specs,
        scratch_shapes=[pltpu.VMEM((LN_GROUPS * SUBLANES, D_MODEL), F32)] if next_mod is not None else [],
        compiler_params=_params(("arbitrary", "arbitrary")),
        name="matmul_layernorm",
    )(*args)
    return (outs[0], outs[1]) if next_mod is not None else (outs[0], None)


def _ffn_up_kernel(*refs, tiles_per_seq, n_tiles, per_step):
    a_ref, prev_ref, next_ref = refs[:3]
    tile_refs = [refs[3 + 6 * t:9 + 6 * t] for t in range(per_step)]
    o_ref, aext_ref = refs[3 + 6 * per_step:]
    i, j = pl.program_id(0), pl.program_id(1)
    tm = a_ref.shape[0]
    ext = tm + 2 * HALO

    @pl.when(j == 0)
    def _():
        first = (i % tiles_per_seq) == 0
        last = (i % tiles_per_seq) == tiles_per_seq - 1
        zero = jnp.zeros(prev_ref.shape, prev_ref.dtype)
        aext_ref[0:HALO, :] = jnp.where(first, zero, prev_ref[...])
        aext_ref[HALO:HALO + tm, :] = a_ref[...]
        aext_ref[HALO + tm:, :] = jnp.where(last, zero, next_ref[...])

    a = aext_ref[...]
    us = [(jnp.dot(a, wg[...], preferred_element_type=F32), jnp.dot(a, wv[...], preferred_element_type=F32))
          for wg, wv, *_ in tile_refs]

    def conv(u, cw_ref, cb_ref):
        up = pltpu.roll(u, 1, 0)[HALO:HALO + tm]
        un = pltpu.roll(u, ext - 1, 0)[HALO:HALO + tm]
        return cw_ref[0:1, :] * up + cw_ref[1:2, :] * u[HALO:HALO + tm] + cw_ref[2:3, :] * un + cb_ref[...]

    for t, ((ug, uv), (_, _, cwg, cwv, cbg, cbv)) in enumerate(zip(us, tile_refs)):
        tn = ug.shape[1]
        act = _silu(conv(ug, cwg, cbg)) * conv(uv, cwv, cbv)
        act = jnp.where(j * per_step + t < n_tiles, act, 0.0)
        o_ref[:, t * tn:(t + 1) * tn] = act.astype(o_ref.dtype)


def ffn_up(h, w_up, conv_w, conv_b, seq_len, tm, tn=FFN_TN, per_step=2):
    m, k = h.shape
    n_tiles = D_FF // tn
    assert seq_len % tm == 0 and D_FF % tn == 0 and D_FF_PAD % (per_step * tn) == 0 and tm % HALO == 0
    hb = tm // HALO
    last_hb = m // HALO - 1
    cb = conv_b.reshape(1, 2 * D_FF)
    in_specs = [pl.BlockSpec((tm, k), lambda i, j: (i, 0)),
                pl.BlockSpec((HALO, k), lambda i, j: (jnp.maximum(i * hb - 1, 0), 0)),
                pl.BlockSpec((HALO, k), lambda i, j: (jnp.minimum((i + 1) * hb, last_hb), 0))]
    args = [h, h, h]
    for t in range(per_step):
        tc = lambda j, t=t: jnp.minimum(j * per_step + t, n_tiles - 1)
        in_specs += [pl.BlockSpec((k, tn), lambda i, j, tc=tc: (0, tc(j))),
                     pl.BlockSpec((k, tn), lambda i, j, tc=tc: (0, n_tiles + tc(j))),
                     pl.BlockSpec((CONV_W, tn), lambda i, j, tc=tc: (0, tc(j))),
                     pl.BlockSpec((CONV_W, tn), lambda i, j, tc=tc: (0, n_tiles + tc(j))),
                     pl.BlockSpec((1, tn), lambda i, j, tc=tc: (0, tc(j))),
                     pl.BlockSpec((1, tn), lambda i, j, tc=tc: (0, n_tiles + tc(j)))]
        args += [w_up, w_up, conv_w, conv_w, cb, cb]
    return pl.pallas_call(
        functools.partial(_ffn_up_kernel, tiles_per_seq=seq_len // tm, n_tiles=n_tiles, per_step=per_step),
        out_shape=jax.ShapeDtypeStruct((m, D_FF_PAD), BF16),
        grid=(m // tm, D_FF_PAD // (per_step * tn)),
        in_specs=in_specs,
        out_specs=pl.BlockSpec((tm, per_step * tn), lambda i, j: (i, j)),
        scratch_shapes=[pltpu.VMEM((tm + 2 * HALO, k), BF16)],
        compiler_params=_params(("arbitrary", "arbitrary")),
        name="ffn_up",
    )(*args)


def _rope_tables():
    rows = SEQ // GRID_W
    row = jnp.repeat(jnp.arange(rows, dtype=F32), GRID_W)
    col = jnp.tile(jnp.arange(GRID_W, dtype=F32), rows)
    half = ROPE_AXIS_DIM // 2
    inv = ROPE_THETA ** (-jnp.arange(half, dtype=F32) / half)
    ang_r, ang_c = row[:, None] * inv, col[:, None] * inv
    ang = jnp.concatenate([ang_r, ang_r, ang_c, ang_c], axis=-1)
    sign = jnp.tile(jnp.concatenate([-jnp.ones((half,), F32), jnp.ones((half,), F32)]), 2)
    return jnp.cos(ang), jnp.sin(ang) * sign


def _lat_group(i, tm):
    return i // (SEQ // tm)


def _ctx_group(i, tm):
    return CTX_GROUP


def _ffn(h_l, h_c, w_up, conv_w, conv_b):
    f_l = ffn_up(h_l, w_up, conv_w, conv_b, SEQ, tm=1024)
    f_c = ffn_up(h_c, w_up, conv_w, conv_b, CTX_LEN, tm=CTX_LEN) if h_c is not None else None
    return f_l, f_c


def kernel(x, c, ctx, c_ctx, hgrn_lb_f, hgrn_lb_b, l0_w_mod, l0_b_mod, l0_w_in, l0_gla_wg_f, l0_gla_bg_f, l0_gla_wg_b, l0_gla_bg_b, l0_da_lq1, l0_da_lk1, l0_da_lq2, l0_da_lk2, l0_da_norm, l0_gla_norm, l0_w_out, l0_ln1_g, l0_ln1_b, l0_ffn_up, l0_ffn_conv, l0_ffn_conv_b, l0_ffn_down, l0_ln2_g, l0_ln2_b, l1_w_mod, l1_b_mod, l1_w_in, l1_hg_norm, l1_w_out, l1_ln1_g, l1_ln1_b, l1_ffn_up, l1_ffn_conv, l1_ffn_conv_b, l1_ffn_down, l1_ln2_g, l1_ln2_b):
    xl = x.reshape(BATCH * SEQ, D_MODEL)
    xc = ctx.reshape(BATCH * CTX_LEN, D_MODEL)
    cond = jnp.concatenate([c, c_ctx[None, :], jnp.zeros((MOD_ROWS - BATCH - 1, D_MODEL), F32)], axis=0)
    mods0 = modulation_table(cond, l0_w_mod, l0_b_mod)
    mods1 = modulation_table(cond, l1_w_mod, l1_b_mod)
    bf = lambda w: w.astype(BF16)
    down = lambda w: convert_pad_rows(w, D_FF_PAD)
    tm_l, tm_c = 512, 256
    pm_l, pm_c = 512, 256

    w_in = l0_w_in.T
    h_l = modulate(xl, mods0, _lat_group)
    h_c = modulate(xc, mods0, _ctx_group)
    q_cols = DA_HEADS * 2 * DA_DK
    pa_l = project(h_l, w_in, 0, ATTN_COLS, BF16, pm_l, q_cols=q_cols, rope=_rope_tables(), rope_cols=2 * q_cols,
                   w_transposed=True)
    pa_c = project(h_c, w_in, 0, ATTN_COLS, BF16, pm_c, q_cols=q_cols, w_transposed=True)
    pb_l = project(h_l, w_in, ATTN_COLS, GLA_COLS, BF16, pm_l, w_transposed=True)
    pb_c = project(h_c, w_in, ATTN_COLS, GLA_COLS, BF16, pm_c, w_transposed=True)
    w_r = jnp.pad(l0_w_in[:, ATTN_COLS + GLA_COLS:], ((0, 0), (0, LANES - 2 * GLA_GATE_RANK)))
    pr_l = project(h_l, w_r, 0, LANES, F32, pm_l, tn=LANES)
    pr_c = project(h_c, w_r, 0, LANES, F32, pm_c, tn=LANES)

    lam_vecs = (l0_da_lq1, l0_da_lk1, l0_da_lq2, l0_da_lk2)
    oa_l = diff_attention(pa_l, [(pa_l, SEQ), (pa_c, CTX_LEN)], lam_vecs, l0_da_norm, SEQ, tq=512)
    oa_c = diff_attention(pa_c, [(pa_c, CTX_LEN)], lam_vecs, l0_da_norm, CTX_LEN, tq=CTX_LEN)

    def gate_weight(wg, row0):
        w = bf(wg).reshape(GLA_GATE_RANK, GLA_HEADS, GLA_DK).transpose(1, 0, 2)
        return jnp.pad(w, ((0, 0), (row0, LANES - GLA_GATE_RANK - row0), (0, 0)))

    gla_extras = (gate_weight(l0_gla_wg_f, 0), gate_weight(l0_gla_wg_b, GLA_GATE_RANK),
                  l0_gla_bg_f.reshape(GLA_HEADS, 1, GLA_DK), l0_gla_bg_b.reshape(GLA_HEADS, 1, GLA_DK), l0_gla_norm)
    ob_l, ob_c = bidir_scan("gla", (pb_l, pr_l), (pb_c, pr_c), gla_extras, emit_ctx=True)

    w_out = bf(l0_w_out)
    x1_l, hf_l = matmul_layernorm(jnp.concatenate([oa_l, ob_l], axis=1), w_out, xl, (mods0, 2), l0_ln1_g, l0_ln1_b,
                                  (mods0, 3, 4), _lat_group, tm=tm_l)
    x1_c, hf_c = matmul_layernorm(jnp.concatenate([oa_c, ob_c], axis=1), w_out, xc, (mods0, 2), l0_ln1_g, l0_ln1_b,
                                  (mods0, 3, 4), _ctx_group, tm=tm_c)
    f_l, f_c = _ffn(hf_l, hf_c, bf(l0_ffn_up), l0_ffn_conv, l0_ffn_conv_b)
    w_down = down(l0_ffn_down)
    x2_l, h1_l = matmul_layernorm(f_l, w_down, x1_l, (mods0, 5), l0_ln2_g, l0_ln2_b, (mods1, 0, 1), _lat_group, tm=tm_l)
    _, h1_c = matmul_layernorm(f_c, w_down, x1_c, (mods0, 5), l0_ln2_g, l0_ln2_b, (mods1, 0, 1), _ctx_group, tm=tm_c)

    w_in1 = l1_w_in

    def project1(h, tm):
        return (project(h, w_in1, 0, 2 * D_MODEL, BF16, tm), project(h, w_in1, 2 * D_MODEL, 2 * D_MODEL, F32, tm),
                project(h, w_in1, 4 * D_MODEL, D_MODEL, BF16, tm))

    o1_l, _ = bidir_scan("hgrn", project1(h1_l, pm_l), project1(h1_c, pm_c), (hgrn_lb_f, hgrn_lb_b, l1_hg_norm),
                         emit_ctx=False)
    x3_l, hf1_l = matmul_layernorm(o1_l, bf(l1_w_out), x2_l, (mods1, 2), l1_ln1_g, l1_ln1_b, (mods1, 3, 4),
                                   _lat_group, tm=tm_l)
    f1_l, _ = _ffn(hf1_l, None, bf(l1_ffn_up), l1_ffn_conv, l1_ffn_conv_b)
    x4_l, _ = matmul_layernorm(f1_l, down(l1_ffn_down), x3_l, (mods1, 5), l1_ln2_g, l1_ln2_b, None, _lat_group, tm=tm_l)
    return x4_l.reshape(BATCH, SEQ, D_MODEL)
```
